```python
import math
import jax, jax.numpy as jnp
from jax import lax
import numpy as np

D_MODEL = 1024
BATCH = 8
SEQ = 4096
DEPTH = 2

D_FF = 2816
HEAD_DIM = 64
W_A = 256
A_BLOCKS = 4
A_BLOCK_W = W_A // A_BLOCKS
LRU_C = 8.0
LRU_CONV = 4
N_Q_HEADS = 8
N_KV_HEADS = 2
W_B = N_Q_HEADS * HEAD_DIM
WINDOW = 128
BLK = 128
W_C = 256
C_GROUPS = 4
C_CONV = 31
D_MIX = W_A + W_B + W_C
OFF_LRU_X = 0
OFF_LRU_GATE = OFF_LRU_X + W_A
OFF_Q = OFF_LRU_GATE + W_A
OFF_K = OFF_Q + W_B
OFF_V = OFF_K + N_KV_HEADS * HEAD_DIM
OFF_GLU = OFF_V + N_KV_HEADS * HEAD_DIM
D_IN_PROJ = OFF_GLU + 2 * W_C
NORM_EPS = 1e-6
LN_EPS = 1e-5
NEG_BIG = -1e30

kernel_name = "hymba_style_lru_swa_conformer_macaron"


def rms_norm(x, g):
    xf = x.astype(jnp.float32)
    y = xf * lax.rsqrt(jnp.mean(xf * xf, axis=-1, keepdims=True) + NORM_EPS)
    return (y * g.astype(jnp.float32)).astype(x.dtype)


def layer_norm(x, g, b):
    xf = x.astype(jnp.float32)
    mu = jnp.mean(xf, axis=-1, keepdims=True)
    xc = xf - mu
    var = jnp.mean(xc * xc, axis=-1, keepdims=True)
    y = xc * lax.rsqrt(var + LN_EPS) * g.astype(jnp.float32) + b.astype(jnp.float32)
    return y.astype(x.dtype)


def swiglu(x, w_gu, w_down):
    g, u = jnp.split(x @ w_gu, 2, axis=-1)
    return (jax.nn.silu(g) * u) @ w_down


def causal_depthwise_conv(x, w, b):
    k = w.shape[0]
    y = lax.conv_general_dilated(
        x, w[:, None, :], window_strides=(1,), padding=[(k - 1, 0)],
        dimension_numbers=("NWC", "WIO", "NWC"), feature_group_count=x.shape[-1])
    return y + b


def rg_lru(x, w_a, b_a, w_x, b_x, lam):
    bsz, s, w = x.shape
    xb = x.reshape(bsz, s, A_BLOCKS, A_BLOCK_W)
    r = jax.nn.sigmoid(jnp.einsum("bshi,hij->bshj", xb, w_a).reshape(bsz, s, w) + b_a)
    i = jax.nn.sigmoid(jnp.einsum("bshi,hij->bshj", xb, w_x).reshape(bsz, s, w) + b_x)
    log_a = -LRU_C * r.astype(jnp.float32) * jax.nn.softplus(-lam.astype(jnp.float32))
    a = jnp.exp(log_a)
    u = jnp.sqrt(-jnp.expm1(2.0 * log_a)) * (i * x).astype(jnp.float32)

    def combine(left, right):
        a1, b1 = left
        a2, b2 = right
        return a1 * a2, a2 * b1 + b2

    _, h = lax.associative_scan(combine, (a, u), axis=1)
    return h.astype(x.dtype)


def sliding_window_attention_sinks(q, k, v, sinks):
    bsz, s, h, d = q.shape
    kvh = k.shape[2]
    grp = h // kvh
    nblk = s // BLK
    qb = q.reshape(bsz, nblk, BLK, kvh, grp, d)

    def banded(t):
        cur = t.reshape(bsz, nblk, BLK, kvh, d)
        prev = jnp.pad(t, ((0, 0), (BLK, 0), (0, 0), (0, 0)))[:, :s].reshape(bsz, nblk, BLK, kvh, d)
        return jnp.concatenate([prev, cur], axis=2)

    kw, vw = banded(k), banded(v)
    scores = jnp.einsum("bnqkgd,bnjkd->bnkgqj", qb, kw).astype(jnp.float32) * (1.0 / math.sqrt(d))
    qi = jnp.arange(BLK)[:, None]
    kj = jnp.arange(2 * BLK)[None, :]
    rel = BLK + qi - kj
    k_pos = (jnp.arange(nblk)[:, None, None] - 1) * BLK + kj[None]
    mask = (rel >= 0)[None] & (rel < WINDOW)[None] & (k_pos >= 0)
    scores = jnp.where(mask[None, :, None, None], scores, NEG_BIG)
    sink = sinks.astype(jnp.float32).reshape(1, 1, kvh, grp, 1, 1)
    m = jnp.maximum(jnp.max(scores, axis=-1, keepdims=True), sink)
    p = jnp.exp(scores - m)
    p = p / (jnp.sum(p, axis=-1, keepdims=True) + jnp.exp(sink - m))
    o = jnp.einsum("bnkgqj,bnjkd->bnqkgd", p.astype(v.dtype), vw)
    return o.reshape(bsz, s, h * d)


def conformer_conv(glu_in, w, b, ln_g, ln_b):
    a, g = jnp.split(glu_in, 2, axis=-1)
    y = a * jax.nn.sigmoid(g)
    y = causal_depthwise_conv(y, w, b)
    y = layer_norm(y, ln_g, ln_b)
    return jax.nn.silu(y)


def setup_inputs(seed: int = 0) -> dict:
    key = jax.random.key(seed)
    ks = iter(jax.random.split(key, 40))
    L = DEPTH

    def nrm(shape, scale):
        return scale * jax.random.normal(next(ks), shape, jnp.float32)

    def gain(shape):
        return 1.0 + 0.05 * jax.random.normal(next(ks), shape, jnp.float32)

    a0 = jax.random.uniform(next(ks), (L, W_A), jnp.float32, minval=0.9, maxval=0.999)
    return {
        "x": jax.random.normal(next(ks), (BATCH, SEQ, D_MODEL), jnp.float32),
        "ffn1_pre_g": gain((L, D_MODEL)),
        "ffn1_w_gu": nrm((L, D_MODEL, 2 * D_FF), D_MODEL ** -0.5),
        "ffn1_w_down": nrm((L, D_FF, D_MODEL), D_FF ** -0.5),
        "ffn1_post_g": gain((L, D_MODEL)),
        "mix_pre_g": gain((L, D_MODEL)),
        "w_in": nrm((L, D_MODEL, D_IN_PROJ), D_MODEL ** -0.5),
        "lru_conv_w": nrm((L, LRU_CONV, W_A), LRU_CONV ** -0.5),
        "lru_conv_b": nrm((L, W_A), 0.02),
        "lru_w_a": nrm((L, A_BLOCKS, A_BLOCK_W, A_BLOCK_W), A_BLOCK_W ** -0.5),
        "lru_b_a": nrm((L, W_A), 0.02),
        "lru_w_x": nrm((L, A_BLOCKS, A_BLOCK_W, A_BLOCK_W), A_BLOCK_W ** -0.5),
        "lru_b_x": nrm((L, W_A), 0.02),
        "lru_lambda": jnp.log(a0) - jnp.log1p(-a0),
        "attn_sinks": nrm((L, N_Q_HEADS), 0.5),
        "conv_w": nrm((L, C_CONV, W_C), C_CONV ** -0.5),
        "conv_b": nrm((L, W_C), 0.02),
        "conv_ln_g": gain((L, W_C)),
        "conv_ln_b": nrm((L, W_C), 0.02),
        "group_g": gain((L, D_MIX)),
        "w_out": nrm((L, D_MIX, D_MODEL), D_MIX ** -0.5),
        "mix_post_g": gain((L, D_MODEL)),
        "ffn2_pre_g": gain((L, D_MODEL)),
        "ffn2_w_gu": nrm((L, D_MODEL, 2 * D_FF), D_MODEL ** -0.5),
        "ffn2_w_down": nrm((L, D_FF, D_MODEL), D_FF ** -0.5),
        "ffn2_post_g": gain((L, D_MODEL)),
    }


def reference(x, ffn1_pre_g, ffn1_w_gu, ffn1_w_down, ffn1_post_g, mix_pre_g, w_in,
              lru_conv_w, lru_conv_b, lru_w_a, lru_b_a, lru_w_x, lru_b_x, lru_lambda,
              attn_sinks, conv_w, conv_b, conv_ln_g, conv_ln_b, group_g, w_out,
              mix_post_g, ffn2_pre_g, ffn2_w_gu, ffn2_w_down, ffn2_post_g):
    bsz, s, _ = x.shape
    for l in range(DEPTH):
        x = x + 0.5 * rms_norm(swiglu(rms_norm(x, ffn1_pre_g[l]), ffn1_w_gu[l], ffn1_w_down[l]), ffn1_post_g[l])

        hn = rms_norm(x, mix_pre_g[l])
        proj = hn @ w_in[l]
        lru_x = proj[..., OFF_LRU_X:OFF_LRU_GATE]
        lru_gate = proj[..., OFF_LRU_GATE:OFF_Q]
        q = proj[..., OFF_Q:OFF_K].reshape(bsz, s, N_Q_HEADS, HEAD_DIM)
        k = proj[..., OFF_K:OFF_V].reshape(bsz, s, N_KV_HEADS, HEAD_DIM)
        v = proj[..., OFF_V:OFF_GLU].reshape(bsz, s, N_KV_HEADS, HEAD_DIM)
        glu_in = proj[..., OFF_GLU:]

        y_a = jax.nn.gelu(lru_gate) * rg_lru(
            causal_depthwise_conv(lru_x, lru_conv_w[l], lru_conv_b[l]),
            lru_w_a[l], lru_b_a[l], lru_w_x[l], lru_b_x[l], lru_lambda[l])
        y_b = sliding_window_attention_sinks(q, k, v, attn_sinks[l])
        y_c = conformer_conv(glu_in, conv_w[l], conv_b[l], conv_ln_g[l], conv_ln_b[l])

        gg = group_g[l]
        y = jnp.concatenate([
            rms_norm(y_a, gg[:W_A]),
            rms_norm(y_b, gg[W_A:W_A + W_B]),
            rms_norm(y_c, gg[W_A + W_B:]),
        ], axis=-1)
        x = x + rms_norm(y @ w_out[l], mix_post_g[l])

        x = x + 0.5 * rms_norm(swiglu(rms_norm(x, ffn2_pre_g[l]), ffn2_w_gu[l], ffn2_w_down[l]), ffn2_post_g[l])
    return x
```

```python
import functools
import math

import jax
import jax.numpy as jnp
from jax import lax
from jax.experimental import pallas as pl
from jax.experimental.pallas import tpu as pltpu

D_MODEL = 1024
D_FF = 2816
HEAD_DIM = 64
W_A = 256
A_BLOCKS = 4
A_BLOCK_W = W_A // A_BLOCKS
LRU_C = 8.0
LRU_CONV = 4
N_Q_HEADS = 8
N_KV_HEADS = 2
W_B = N_Q_HEADS * HEAD_DIM
WINDOW = 128
BLK = 128
W_C = 256
C_CONV = 31
D_MIX = W_A + W_B + W_C
OFF_Q = 2 * W_A
OFF_K = OFF_Q + W_B
OFF_V = OFF_K + N_KV_HEADS * HEAD_DIM
OFF_GLU = OFF_V + N_KV_HEADS * HEAD_DIM
D_IN_PROJ = OFF_GLU + 2 * W_C
NORM_EPS = 1e-6
LN_EPS = 1e-5
NEG_BIG = -1e30

SUBLANES = 8
LANES = 128
MXU_DIM = 256

FF_CHUNK = MXU_DIM
N_FF_CHUNKS = D_FF // FF_CHUNK
LX_HALO = SUBLANES
GLU_HALO = 4 * SUBLANES
VMEM_LIMIT_BYTES = 52 * 1024 * 1024

F32 = jnp.float32
BF16 = jnp.bfloat16


def _rms(x, g):
    return x * lax.rsqrt(jnp.mean(x * x, axis=-1, keepdims=True) + NORM_EPS) * g


def _dot(a, b):
    return jnp.dot(a, b, preferred_element_type=F32)


def _const_spec(shape):
    zeros = (0,) * len(shape)
    return pl.BlockSpec(shape, lambda *_: zeros, pipeline_mode=pl.Buffered(1))


def _ffn_kernel(x_ref, pre_ref, wgu_ref, wd_ref, post_ref, o_ref, act_ref):
    x = x_ref[...]
    xn = _rms(x, pre_ref[...]).astype(BF16)
    for j in range(N_FF_CHUNKS):
        gu = _dot(xn, wgu_ref[:, j * 2 * FF_CHUNK:(j + 1) * 2 * FF_CHUNK])
        g = gu[:, :FF_CHUNK]
        u = gu[:, FF_CHUNK:]
        act_ref[:, j * FF_CHUNK:(j + 1) * FF_CHUNK] = (g * jax.nn.sigmoid(g) * u).astype(BF16)
    out = _dot(act_ref[...], wd_ref[...])
    o_ref[...] = x + 0.5 * _rms(out, post_ref[...])


def _ffn_call(x2d, pre_g, w_gu, w_down, post_g, tm):
    n = x2d.shape[0]
    return pl.pallas_call(
        _ffn_kernel,
        out_shape=jax.ShapeDtypeStruct(x2d.shape, F32),
        grid=(n // tm,),
        in_specs=[
            pl.BlockSpec((tm, D_MODEL), lambda i: (i, 0)),
            _const_spec((1, D_MODEL)),
            _const_spec((D_MODEL, 2 * D_FF)),
            _const_spec((D_FF, D_MODEL)),
            _const_spec((1, D_MODEL)),
        ],
        out_specs=pl.BlockSpec((tm, D_MODEL), lambda i: (i, 0)),
        scratch_shapes=[pltpu.VMEM((tm, D_FF), BF16)],
        compiler_params=pltpu.CompilerParams(
            dimension_semantics=("arbitrary",), vmem_limit_bytes=VMEM_LIMIT_BYTES),
        name="ffn",
    )(x2d, pre_g, w_gu, w_down, post_g)


def _lru_scan(a, u, h0, a_s, b_s):
    t = a.shape[0]
    row = lax.broadcasted_iota(jnp.int32, a.shape, 0) & (SUBLANES - 1)
    for k in (1, 2, 4):
        keep = row >= k
        a_sh = jnp.where(keep, pltpu.roll(a, k, axis=0), 1.0)
        u_sh = jnp.where(keep, pltpu.roll(u, k, axis=0), 0.0)
        u = a * u_sh + u
        a = a * a_sh
    a_s[...] = a
    b_s[...] = u

    def body(g, h):
        i = pl.multiple_of(g * SUBLANES, SUBLANES)
        hg = a_s[pl.ds(i, SUBLANES), :] * h + b_s[pl.ds(i, SUBLANES), :]
        b_s[pl.ds(i, SUBLANES), :] = hg
        return hg[SUBLANES - 1:SUBLANES, :]

    h_last = lax.fori_loop(0, t // SUBLANES, body, h0, unroll=8)
    return b_s[...], h_last


def _gelu_tanh(x):
    return 0.5 * x * (1.0 + jnp.tanh(math.sqrt(2.0 / math.pi) * (x + 0.044715 * (x * x * x))))


def _mixer_kernel(sinks_ref, x_ref, pre_ref, w_in_ref, lcw_ref, lcb_ref, wg_ref, bg_ref, lam_ref,
                  cw_ref, cb_ref, lng_ref, lnb_ref, gg_ref, w_out_ref, post_ref, o_ref,
                  lx_ext, glu_ext, kx, vx, h_ref, a_s, b_s, yb_ref):
    t_idx = pl.program_id(1)
    tt = x_ref.shape[0]

    @pl.when(t_idx == 0)
    def _():
        lx_ext[0:LX_HALO, :] = jnp.zeros((LX_HALO, W_A), F32)
        glu_ext[0:GLU_HALO, :] = jnp.zeros((GLU_HALO, W_C), F32)
        kx[:, 0:BLK, :] = jnp.zeros((4, BLK, LANES), BF16)
        vx[:, 0:BLK, :] = jnp.zeros((4, BLK, LANES), BF16)
        h_ref[...] = jnp.zeros((1, W_A), F32)

    x = x_ref[...]
    hn = _rms(x, pre_ref[...]).astype(BF16)

    pa = _dot(hn, w_in_ref[:, 0:OFF_Q])
    gate = pa[:, W_A:]
    lx_ext[LX_HALO:LX_HALO + tt, :] = pa[:, :W_A]
    xc = lcb_ref[...]
    for k in range(LRU_CONV):
        xc = xc + lcw_ref[k:k + 1, :] * lx_ext[pl.ds(LX_HALO - (LRU_CONV - 1) + k, tt), :]
    lx_ext[0:LX_HALO, :] = lx_ext[tt:tt + LX_HALO, :]
    gr = _dot(xc.astype(BF16), wg_ref[...]) + bg_ref[...]
    r = jax.nn.sigmoid(gr[:, :W_A])
    i_gate = jax.nn.sigmoid(gr[:, W_A:])
    z = -lam_ref[...]
    softplus = jnp.maximum(z, 0.0) + jnp.log1p(jnp.exp(-jnp.abs(z)))
    log_a = (-LRU_C) * r * softplus
    a = jnp.exp(log_a)
    th = jnp.tanh(log_a)
    u = jnp.sqrt(-2.0 * th / (1.0 - th)) * (i_gate * xc)
    h, h_last = _lru_scan(a, u, h_ref[...], a_s, b_s)
    h_ref[...] = h_last
    y_a = _gelu_tanh(gate) * h

    qkv = _dot(hn, w_in_ref[:, OFF_Q:OFF_GLU])
    q = (qkv[:, :W_B] * (1.0 / math.sqrt(HEAD_DIM))).astype(BF16)
    lane = lax.broadcasted_iota(jnp.int32, (tt, LANES), 1)
    lo = lane < HEAD_DIM
    for src, dst in ((qkv[:, W_B:W_B + LANES], kx), (qkv[:, W_B + LANES:], vx)):
        rot = pltpu.roll(src, HEAD_DIM, axis=1)
        dst[0, BLK:BLK + tt, :] = jnp.where(lo, src, 0.0).astype(BF16)
        dst[1, BLK:BLK + tt, :] = jnp.where(lo, 0.0, rot).astype(BF16)
        dst[2, BLK:BLK + tt, :] = jnp.where(lo, rot, 0.0).astype(BF16)
        dst[3, BLK:BLK + tt, :] = jnp.where(lo, 0.0, src).astype(BF16)

    qi = lax.broadcasted_iota(jnp.int32, (2 * BLK, 2 * BLK), 0) & (BLK - 1)
    kj = lax.broadcasted_iota(jnp.int32, (2 * BLK, 2 * BLK), 1)
    rel = BLK + qi - kj
    band = (rel >= 0) & (rel < WINDOW)
    top = lax.broadcasted_iota(jnp.int32, (2 * BLK, 1), 0) < BLK
    for j in range(tt // BLK):
        if j == 0:
            mask = band & ((kj + (t_idx * tt - BLK)) >= 0)
        else:
            mask = band
        for g in range(N_KV_HEADS):
            p0 = 2 * g * LANES
            lhs = jnp.concatenate([q[j * BLK:(j + 1) * BLK, p0:p0 + LANES],
                                   q[j * BLK:(j + 1) * BLK, p0 + LANES:p0 + 2 * LANES]], axis=0)
            o = jnp.zeros((2 * BLK, LANES), F32)
            for e in range(2):
                kk = kx[2 * g + e, j * BLK:(j + 2) * BLK, :]
                vv = vx[2 * g + e, j * BLK:(j + 2) * BLK, :]
                s = lax.dot_general(lhs, kk, (((1,), (1,)), ((), ())), preferred_element_type=F32)
                s = jnp.where(mask, s, NEG_BIG)
                sink = jnp.where(top, sinks_ref[4 * g + e], sinks_ref[4 * g + 2 + e])
                m = jnp.maximum(jnp.max(s, axis=-1, keepdims=True), sink)
                p = jnp.exp(s - m)
                denom = jnp.sum(p, axis=-1, keepdims=True) + jnp.exp(sink - m)
                o = o + _dot(p.astype(BF16), vv) * (1.0 / denom)
            yb_ref[j * BLK:(j + 1) * BLK, p0:p0 + LANES] = o[:BLK]
            yb_ref[j * BLK:(j + 1) * BLK, p0 + LANES:p0 + 2 * LANES] = o[BLK:]
    kx[:, 0:BLK, :] = kx[:, tt:tt + BLK, :]
    vx[:, 0:BLK, :] = vx[:, tt:tt + BLK, :]

    pc = _dot(hn, w_in_ref[:, OFF_GLU:])
    glu_ext[GLU_HALO:GLU_HALO + tt, :] = pc[:, :W_C] * jax.nn.sigmoid(pc[:, W_C:])
    yc = cb_ref[...]
    for k in range(C_CONV):
        yc = yc + cw_ref[k:k + 1, :] * glu_ext[pl.ds(GLU_HALO - (C_CONV - 1) + k, tt), :]
    glu_ext[0:GLU_HALO, :] = glu_ext[tt:tt + GLU_HALO, :]
    mu = jnp.mean(yc, axis=-1, keepdims=True)
    ycc = yc - mu
    var = jnp.mean(ycc * ycc, axis=-1, keepdims=True)
    yc = ycc * lax.rsqrt(var + LN_EPS) * lng_ref[...] + lnb_ref[...]
    y_c = yc * jax.nn.sigmoid(yc)

    gg = gg_ref[...]
    ycat = jnp.concatenate([
        _rms(y_a, gg[:, :W_A]),
        _rms(yb_ref[...], gg[:, W_A:W_A + W_B]),
        _rms(y_c, gg[:, W_A + W_B:]),
    ], axis=-1).astype(BF16)
    o_ref[...] = x + _rms(_dot(ycat, w_out_ref[...]), post_ref[...])


def _mixer_call(x, sinks, pre_g, w_in, lcw, lcb, wg, bg, lam, cw, cb, lng, lnb, gg, w_out, post_g, tt):
    bsz, s, _ = x.shape
    vec = lambda n: _const_spec((1, n))
    return pl.pallas_call(
        _mixer_kernel,
        out_shape=jax.ShapeDtypeStruct(x.shape, F32),
        grid=(bsz, s // tt),
        in_specs=[
            pl.BlockSpec(memory_space=pltpu.SMEM),
            pl.BlockSpec((None, tt, D_MODEL), lambda b, t: (b, t, 0)),
            vec(D_MODEL),
            _const_spec((D_MODEL, D_IN_PROJ)),
            _const_spec((LRU_CONV, W_A)),
            vec(W_A),
            _const_spec((W_A, 2 * W_A)),
            vec(2 * W_A),
            vec(W_A),
            _const_spec((C_CONV, W_C)),
            vec(W_C),
            vec(W_C),
            vec(W_C),
            vec(D_MIX),
            _const_spec((D_MIX, D_MODEL)),
            vec(D_MODEL),
        ],
        out_specs=pl.BlockSpec((None, tt, D_MODEL), lambda b, t: (b, t, 0)),
        scratch_shapes=[
            pltpu.VMEM((LX_HALO + tt, W_A), F32),
            pltpu.VMEM((GLU_HALO + tt, W_C), F32),
            pltpu.VMEM((4, BLK + tt, LANES), BF16),
            pltpu.VMEM((4, BLK + tt, LANES), BF16),
            pltpu.VMEM((1, W_A), F32),
            pltpu.VMEM((tt, W_A), F32),
            pltpu.VMEM((tt, W_A), F32),
            pltpu.VMEM((tt, W_B), F32),
        ],
        compiler_params=pltpu.CompilerParams(
            dimension_semantics=("arbitrary", "arbitrary"), vmem_limit_bytes=VMEM_LIMIT_BYTES),
        name="mixer",
    )(sinks, x, pre_g, w_in, lcw, lcb, wg, bg, lam, cw, cb, lng, lnb, gg, w_out, post_g)


def _interleave_gu(w_gu):
    g = w_gu[:, :D_FF].reshape(D_MODEL, N_FF_CHUNKS, 1, FF_CHUNK)
    u = w_gu[:, D_FF:].reshape(D_MODEL, N_FF_CHUNKS, 1, FF_CHUNK)
    return jnp.concatenate([g, u], axis=2).reshape(D_MODEL, 2 * D_FF).astype(BF16)


def _block_diag(w):
    eye = jnp.eye(A_BLOCKS, dtype=w.dtype)
    return jnp.einsum("hij,hg->higj", w, eye).reshape(W_A, W_A)


def _pick_tile(n, target):
    t = min(n, target)
    while n % t:
        t //= 2
    return t


def kernel(x, ffn1_pre_g, ffn1_w_gu, ffn1_w_down, ffn1_post_g, mix_pre_g, w_in, lru_conv_w, lru_conv_b, lru_w_a, lru_b_a, lru_w_x, lru_b_x, lru_lambda, attn_sinks, conv_w, conv_b, conv_ln_g, conv_ln_b, group_g, w_out, mix_post_g, ffn2_pre_g, ffn2_w_gu, ffn2_w_down, ffn2_post_g):
    bsz, s, d = x.shape
    depth = w_in.shape[0]
    tm = _pick_tile(bsz * s, 512)
    tt = _pick_tile(s, 512)
    row = lambda v: v.reshape(1, -1)
    for l in range(depth):
        x = _ffn_call(x.reshape(bsz * s, d), row(ffn1_pre_g[l]), _interleave_gu(ffn1_w_gu[l]),
                      ffn1_w_down[l].astype(BF16), row(ffn1_post_g[l]), tm).reshape(bsz, s, d)
        w_gates = jnp.concatenate([_block_diag(lru_w_a[l]), _block_diag(lru_w_x[l])], axis=1).astype(BF16)
        b_gates = jnp.concatenate([lru_b_a[l], lru_b_x[l]]).reshape(1, -1)
        x = _mixer_call(x, attn_sinks[l], row(mix_pre_g[l]), w_in[l].astype(BF16), lru_conv_w[l],
                        row(lru_conv_b[l]), w_gates, b_gates, row(lru_lambda[l]), conv_w[l],
                        row(conv_b[l]), row(conv_ln_g[l]), row(conv_ln_b[l]), row(group_g[l]),
                        w_out[l].astype(BF16), row(mix_post_g[l]), tt)
        x = _ffn_call(x.reshape(bsz * s, d), row(ffn2_pre_g[l]), _interleave_gu(ffn2_w_gu[l]),
                      ffn2_w_down[l].astype(BF16), row(ffn2_post_g[l]), tm).reshape(bsz, s, d)
    return x
```

```python
import functools
import math

import jax
import jax.numpy as jnp
from jax import lax
from jax.experimental import pallas as pl
from jax.experimental.pallas import tpu as pltpu

D_MODEL = 1024
D_FF = 2816
HEAD_DIM = 64
W_A = 256
A_BLOCKS = 4
A_BLOCK_W = W_A // A_BLOCKS
LRU_C = 8.0
LRU_CONV = 4
N_Q_HEADS = 8
N_KV_HEADS = 2
W_B = N_Q_HEADS * HEAD_DIM
WINDOW = 128
BLK = 128
W_C = 256
C_CONV = 31
D_MIX = W_A + W_B + W_C
OFF_Q = 2 * W_A
OFF_K = OFF_Q + W_B
OFF_V = OFF_K + N_KV_HEADS * HEAD_DIM
OFF_GLU = OFF_V + N_KV_HEADS * HEAD_DIM
D_IN_PROJ = OFF_GLU + 2 * W_C
NORM_EPS = 1e-6
LN_EPS = 1e-5
NEG_BIG = -1e30

SUBLANES = 8
LANES = 128
MXU_DIM = 256

FF_CHUNK = MXU_DIM
N_FF_CHUNKS = D_FF // FF_CHUNK
LX_HALO = SUBLANES
GLU_HALO = 4 * SUBLANES
CONV_ROWS = 64
VMEM_LIMIT_BYTES = 52 * 1024 * 1024

F32 = jnp.float32
BF16 = jnp.bfloat16


def _rms(x, g):
    return x * lax.rsqrt(jnp.mean(x * x, axis=-1, keepdims=True) + NORM_EPS) * g


def _dot(a, b):
    return jnp.dot(a, b, preferred_element_type=F32)


def _layer_spec(layer, shape):
    idx = (layer,) + (0,) * len(shape)
    return pl.BlockSpec((None,) + tuple(shape), lambda *_: idx, pipeline_mode=pl.Buffered(1))


def _ffn_kernel(x_ref, pre_ref, wgu_ref, wd_ref, post_ref, o_ref, act_ref):
    x = x_ref[...]
    xn = _rms(x, pre_ref[...]).astype(BF16)
    for j in range(N_FF_CHUNKS):
        g = _dot(xn, wgu_ref[:, j * FF_CHUNK:(j + 1) * FF_CHUNK])
        u = _dot(xn, wgu_ref[:, D_FF + j * FF_CHUNK:D_FF + (j + 1) * FF_CHUNK])
        act_ref[:, j * FF_CHUNK:(j + 1) * FF_CHUNK] = (g * jax.nn.sigmoid(g) * u).astype(BF16)
    out = _dot(act_ref[...], wd_ref[...])
    o_ref[...] = x + 0.5 * _rms(out, post_ref[...])


def _ffn_call(x2d, layer, pre_g, w_gu, w_down, post_g, tm):
    n = x2d.shape[0]
    return pl.pallas_call(
        _ffn_kernel,
        out_shape=jax.ShapeDtypeStruct(x2d.shape, F32),
        grid=(n // tm,),
        in_specs=[
            pl.BlockSpec((tm, D_MODEL), lambda i: (i, 0)),
            _layer_spec(layer, (1, D_MODEL)),
            _layer_spec(layer, (D_MODEL, 2 * D_FF)),
            _layer_spec(layer, (D_FF, D_MODEL)),
            _layer_spec(layer, (1, D_MODEL)),
        ],
        out_specs=pl.BlockSpec((tm, D_MODEL), lambda i: (i, 0)),
        scratch_shapes=[pltpu.VMEM((tm, D_FF), BF16)],
        compiler_params=pltpu.CompilerParams(
            dimension_semantics=("arbitrary",), vmem_limit_bytes=VMEM_LIMIT_BYTES),
        name="ffn",
    )(x2d, pre_g, w_gu, w_down, post_g)


def _lru_scan(a, u, h0, a_s, b_s):
    t = a.shape[0]
    row = lax.broadcasted_iota(jnp.int32, a.shape, 0) & (SUBLANES - 1)
    for k in (1, 2, 4):
        keep = row >= k
        a_sh = jnp.where(keep, pltpu.roll(a, k, axis=0), 1.0)
        u_sh = jnp.where(keep, pltpu.roll(u, k, axis=0), 0.0)
        u = a * u_sh + u
        a = a * a_sh
    a_s[...] = a
    b_s[...] = u

    def body(g, h):
        i = pl.multiple_of(g * SUBLANES, SUBLANES)
        hg = a_s[pl.ds(i, SUBLANES), :] * h + b_s[pl.ds(i, SUBLANES), :]
        b_s[pl.ds(i, SUBLANES), :] = hg
        return hg[SUBLANES - 1:SUBLANES, :]

    h_last = lax.fori_loop(0, t // SUBLANES, body, h0, unroll=8)
    return b_s[...], h_last


def _residues(halo, n_taps):
    return tuple(sorted({(halo - (n_taps - 1) + k) % SUBLANES for k in range(n_taps)} - {0}))


def _fill_shifted(ext_ref, sh_ref, residues):
    ext = ext_ref[...]
    n_rows = ext.shape[0]
    for n, s in enumerate(residues):
        sh_ref[n] = pltpu.roll(ext, n_rows - s, axis=0)


def _causal_dwconv(ext_ref, sh_ref, residues, w_ref, bias, halo, n_taps, r0, rows):
    acc = bias
    for k in range(n_taps):
        off = halo - (n_taps - 1) + k
        s = off % SUBLANES
        start = r0 + off - s
        if s == 0:
            tap = ext_ref[start:start + rows, :]
        else:
            tap = sh_ref[residues.index(s), start:start + rows, :]
        acc = acc + w_ref[k:k + 1, :] * tap
    return acc


def _gelu_tanh(x):
    return 0.5 * x * (1.0 + jnp.tanh(math.sqrt(2.0 / math.pi) * (x + 0.044715 * (x * x * x))))


def _mixer_kernel(layer, sinks_ref, x_ref, pre_ref, w_in_ref, lcw_ref, lcb_ref, wg_ref, bg_ref, lam_ref,
                  cw_ref, cb_ref, lng_ref, lnb_ref, gg_ref, w_out_ref, post_ref, o_ref,
                  lx_ext, lx_sh, glu_ext, glu_sh, kx, vx, h_ref, a_s, b_s, yb_ref, ycat_ref):
    t_idx = pl.program_id(1)
    tt = x_ref.shape[0]

    @pl.when(t_idx == 0)
    def _():
        lx_ext[0:LX_HALO, :] = jnp.zeros((LX_HALO, W_A), F32)
        glu_ext[0:GLU_HALO, :] = jnp.zeros((GLU_HALO, W_C), F32)
        kx[:, 0:BLK, :] = jnp.zeros((4, BLK, LANES), BF16)
        vx[:, 0:BLK, :] = jnp.zeros((4, BLK, LANES), BF16)
        h_ref[...] = jnp.zeros((1, W_A), F32)

    x = x_ref[...]
    hn = _rms(x, pre_ref[...]).astype(BF16)
    gg = gg_ref[...]

    pa = _dot(hn, w_in_ref[:, 0:OFF_Q])
    gate = pa[:, W_A:]
    lx_ext[LX_HALO:LX_HALO + tt, :] = pa[:, :W_A]
    lx_res = _residues(LX_HALO, LRU_CONV)
    _fill_shifted(lx_ext, lx_sh, lx_res)
    xc = _causal_dwconv(lx_ext, lx_sh, lx_res, lcw_ref, lcb_ref[...], LX_HALO, LRU_CONV, 0, tt)
    lx_ext[0:LX_HALO, :] = lx_ext[tt:tt + LX_HALO, :]
    gr = _dot(xc.astype(BF16), wg_ref[...]) + bg_ref[...]
    r = jax.nn.sigmoid(gr[:, :W_A])
    i_gate = jax.nn.sigmoid(gr[:, W_A:])
    z = -lam_ref[...]
    softplus = jnp.maximum(z, 0.0) + jnp.log1p(jnp.exp(-jnp.abs(z)))
    log_a = (-LRU_C) * r * softplus
    a = jnp.exp(log_a)
    th = jnp.tanh(log_a)
    u = jnp.sqrt(-2.0 * th / (1.0 - th)) * (i_gate * xc)
    h, h_last = _lru_scan(a, u, h_ref[...], a_s, b_s)
    h_ref[...] = h_last
    ycat_ref[:, 0:W_A] = _rms(_gelu_tanh(gate) * h, gg[:, :W_A]).astype(BF16)

    qkv = _dot(hn, w_in_ref[:, OFF_Q:OFF_GLU])
    q = (qkv[:, :W_B] * (1.0 / math.sqrt(HEAD_DIM))).astype(BF16)
    lane = lax.broadcasted_iota(jnp.int32, (tt, LANES), 1)
    lo = lane < HEAD_DIM
    for src, dst in ((qkv[:, W_B:W_B + LANES], kx), (qkv[:, W_B + LANES:], vx)):
        rot = pltpu.roll(src, HEAD_DIM, axis=1)
        dst[0, BLK:BLK + tt, :] = jnp.where(lo, src, 0.0).astype(BF16)
        dst[1, BLK:BLK + tt, :] = jnp.where(lo, 0.0, rot).astype(BF16)
        dst[2, BLK:BLK + tt, :] = jnp.where(lo, rot, 0.0).astype(BF16)
        dst[3, BLK:BLK + tt, :] = jnp.where(lo, 0.0, src).astype(BF16)

    qi = lax.broadcasted_iota(jnp.int32, (2 * BLK, 2 * BLK), 0) & (BLK - 1)
    kj = lax.broadcasted_iota(jnp.int32, (2 * BLK, 2 * BLK), 1)
    rel = BLK + qi - kj
    band = (rel >= 0) & (rel < WINDOW)
    top = lax.broadcasted_iota(jnp.int32, (2 * BLK, 1), 0) < BLK
    for j in range(tt // BLK):
        if j == 0:
            mask = band & ((kj + (t_idx * tt - BLK)) >= 0)
        else:
            mask = band
        for g in range(N_KV_HEADS):
            p0 = 2 * g * LANES
            lhs = jnp.concatenate([q[j * BLK:(j + 1) * BLK, p0:p0 + LANES],
                                   q[j * BLK:(j + 1) * BLK, p0 + LANES:p0 + 2 * LANES]], axis=0)
            o = jnp.zeros((2 * BLK, LANES), F32)
            for e in range(2):
                kk = kx[2 * g + e, j * BLK:(j + 2) * BLK, :]
                vv = vx[2 * g + e, j * BLK:(j + 2) * BLK, :]
                s = lax.dot_general(lhs, kk, (((1,), (1,)), ((), ())), preferred_element_type=F32)
                s = jnp.where(mask, s, NEG_BIG)
                sink = jnp.where(top, sinks_ref[layer, 4 * g + e], sinks_ref[layer, 4 * g + 2 + e])
                m = jnp.maximum(jnp.max(s, axis=-1, keepdims=True), sink)
                p = jnp.exp(s - m)
                denom = jnp.sum(p, axis=-1, keepdims=True) + jnp.exp(sink - m)
                o = o + _dot(p.astype(BF16), vv) * (1.0 / denom)
            yb_ref[j * BLK:(j + 1) * BLK, p0:p0 + LANES] = o[:BLK]
            yb_ref[j * BLK:(j + 1) * BLK, p0 + LANES:p0 + 2 * LANES] = o[BLK:]
    kx[:, 0:BLK, :] = kx[:, tt:tt + BLK, :]
    vx[:, 0:BLK, :] = vx[:, tt:tt + BLK, :]
    ycat_ref[:, W_A:W_A + W_B] = _rms(yb_ref[...], gg[:, W_A:W_A + W_B]).astype(BF16)

    pc = _dot(hn, w_in_ref[:, OFF_GLU:])
    glu_ext[GLU_HALO:GLU_HALO + tt, :] = pc[:, :W_C] * jax.nn.sigmoid(pc[:, W_C:])
    glu_res = _residues(GLU_HALO, C_CONV)
    _fill_shifted(glu_ext, glu_sh, glu_res)
    for r0 in range(0, tt, CONV_ROWS):
        yc = _causal_dwconv(glu_ext, glu_sh, glu_res, cw_ref, cb_ref[...], GLU_HALO, C_CONV, r0, CONV_ROWS)
        mu = jnp.mean(yc, axis=-1, keepdims=True)
        ycc = yc - mu
        var = jnp.mean(ycc * ycc, axis=-1, keepdims=True)
        yc = ycc * lax.rsqrt(var + LN_EPS) * lng_ref[...] + lnb_ref[...]
        y_c = yc * jax.nn.sigmoid(yc)
        ycat_ref[r0:r0 + CONV_ROWS, W_A + W_B:] = _rms(y_c, gg[:, W_A + W_B:]).astype(BF16)
    glu_ext[0:GLU_HALO, :] = glu_ext[tt:tt + GLU_HALO, :]

    o_ref[...] = x + _rms(_dot(ycat_ref[...], w_out_ref[...]), post_ref[...])


def _mixer_call(x, layer, sinks, pre_g, w_in, lcw, lcb, wg, bg, lam, cw, cb, lng, lnb, gg, w_out, post_g, tt):
    bsz, s, _ = x.shape
    vec = lambda n: _layer_spec(layer, (1, n))
    return pl.pallas_call(
        functools.partial(_mixer_kernel, layer),
        out_shape=jax.ShapeDtypeStruct(x.shape, F32),
        grid=(bsz, s // tt),
        in_specs=[
            pl.BlockSpec(memory_space=pltpu.SMEM),
            pl.BlockSpec((None, tt, D_MODEL), lambda b, t: (b, t, 0)),
            vec(D_MODEL),
            _layer_spec(layer, (D_MODEL, D_IN_PROJ)),
            _layer_spec(layer, (LRU_CONV, W_A)),
            vec(W_A),
            _layer_spec(layer, (W_A, 2 * W_A)),
            vec(2 * W_A),
            vec(W_A),
            _layer_spec(layer, (C_CONV, W_C)),
            vec(W_C),
            vec(W_C),
            vec(W_C),
            vec(D_MIX),
            _layer_spec(layer, (D_MIX, D_MODEL)),
            vec(D_MODEL),
        ],
        out_specs=pl.BlockSpec((None, tt, D_MODEL), lambda b, t: (b, t, 0)),
        scratch_shapes=[
            pltpu.VMEM((LX_HALO + tt, W_A), F32),
            pltpu.VMEM((len(_residues(LX_HALO, LRU_CONV)), LX_HALO + tt, W_A), F32),
            pltpu.VMEM((GLU_HALO + tt, W_C), F32),
            pltpu.VMEM((len(_residues(GLU_HALO, C_CONV)), GLU_HALO + tt, W_C), F32),
            pltpu.VMEM((4, BLK + tt, LANES), BF16),
            pltpu.VMEM((4, BLK + tt, LANES), BF16),
            pltpu.VMEM((1, W_A), F32),
            pltpu.VMEM((tt, W_A), F32),
            pltpu.VMEM((tt, W_A), F32),
            pltpu.VMEM((tt, W_B), F32),
            pltpu.VMEM((tt, D_MIX), BF16),
        ],
        compiler_params=pltpu.CompilerParams(
            dimension_semantics=("arbitrary", "arbitrary"), vmem_limit_bytes=VMEM_LIMIT_BYTES),
        name="mixer",
    )(sinks, x, pre_g, w_in, lcw, lcb, wg, bg, lam, cw, cb, lng, lnb, gg, w_out, post_g)


def _block_diag(w):
    eye = jnp.eye(A_BLOCKS, dtype=w.dtype)
    return jnp.einsum("lhij,hg->lhigj", w, eye).reshape(w.shape[0], W_A, W_A)


def _pick_tile(n, target):
    t = min(n, target)
    while n % t:
        t //= 2
    return t


def kernel(x, ffn1_pre_g, ffn1_w_gu, ffn1_w_down, ffn1_post_g, mix_pre_g, w_in, lru_conv_w, lru_conv_b, lru_w_a, lru_b_a, lru_w_x, lru_b_x, lru_lambda, attn_sinks, conv_w, conv_b, conv_ln_g, conv_ln_b, group_g, w_out, mix_post_g, ffn2_pre_g, ffn2_w_gu, ffn2_w_down, ffn2_post_g):
    bsz, s, d = x.shape
    depth = w_in.shape[0]
    tm = _pick_tile(bsz * s, 512)
    tt = _pick_tile(s, 512)
    rows = lambda v: v.reshape(depth, 1, -1)
    bf = lambda w: w.astype(BF16)
    ffn1 = (rows(ffn1_pre_g), bf(ffn1_w_gu), bf(ffn1_w_down), rows(ffn1_post_g))
    ffn2 = (rows(ffn2_pre_g), bf(ffn2_w_gu), bf(ffn2_w_down), rows(ffn2_post_g))
    w_gates = bf(jnp.concatenate([_block_diag(lru_w_a), _block_diag(lru_w_x)], axis=2))
    b_gates = rows(jnp.concatenate([lru_b_a, lru_b_x], axis=1))
    mix = (attn_sinks, rows(mix_pre_g), bf(w_in), lru_conv_w, rows(lru_conv_b), w_gates, b_gates,
           rows(lru_lambda), conv_w, rows(conv_b), rows(conv_ln_g), rows(conv_ln_b), rows(group_g),
           bf(w_out), rows(mix_post_g))
    x2d = x.reshape(bsz * s, d)
    for l in range(depth):
        x2d = _ffn_call(x2d, l, *ffn1, tm)
        x2d = _mixer_call(x2d.reshape(bsz, s, d), l, *mix, tt).reshape(bsz * s, d)
        x2d = _ffn_call(x2d, l, *ffn2, tm)
    return x2d.reshape(bsz, s, d)
```

```python
import functools
import math

import jax
import jax.numpy as jnp
from jax import lax
from jax.experimental import pallas as pl
from jax.experimental.pallas import tpu as pltpu

D_MODEL = 1024
D_FF = 2816
HEAD_DIM = 64
W_A = 256
A_BLOCKS = 4
A_BLOCK_W = W_A // A_BLOCKS
LRU_C = 8.0
LRU_CONV = 4
N_Q_HEADS = 8
N_KV_HEADS = 2
W_B = N_Q_HEADS * HEAD_DIM
WINDOW = 128
BLK = 128
W_C = 256
C_CONV = 31
D_MIX = W_A + W_B + W_C
OFF_Q = 2 * W_A
OFF_K = OFF_Q + W_B
OFF_V = OFF_K + N_KV_HEADS * HEAD_DIM
OFF_GLU = OFF_V + N_KV_HEADS * HEAD_DIM
D_IN_PROJ = OFF_GLU + 2 * W_C
NORM_EPS = 1e-6
LN_EPS = 1e-5
NEG_BIG = -1e30

SUBLANES = 8
LANES = 128
MXU_DIM = 256

FF_CHUNK = MXU_DIM
N_FF_CHUNKS = D_FF // FF_CHUNK
LX_HALO = SUBLANES
GLU_HALO = 4 * SUBLANES
CONV_ROWS = 64
VMEM_LIMIT_BYTES = 52 * 1024 * 1024

F32 = jnp.float32
BF16 = jnp.bfloat16


def _rms(x, g):
    return x * lax.rsqrt(jnp.mean(x * x, axis=-1, keepdims=True) + NORM_EPS) * g


def _dot(a, b):
    return jnp.dot(a, b, preferred_element_type=F32)


def _layer_spec(layer, shape):
    idx = (layer,) + (0,) * len(shape)
    return pl.BlockSpec((None,) + tuple(shape), lambda *_: idx, pipeline_mode=pl.Buffered(1))


def _ffn_kernel(x_ref, pre_ref, wgu_ref, wd_ref, post_ref, o_ref, act_ref):
    x = x_ref[...]
    xn = _rms(x, pre_ref[...]).astype(BF16)
    for j in range(N_FF_CHUNKS):
        g = _dot(xn, wgu_ref[:, j * FF_CHUNK:(j + 1) * FF_CHUNK])
        u = _dot(xn, wgu_ref[:, D_FF + j * FF_CHUNK:D_FF + (j + 1) * FF_CHUNK])
        act_ref[:, j * FF_CHUNK:(j + 1) * FF_CHUNK] = (g * jax.nn.sigmoid(g) * u).astype(BF16)
    out = _dot(act_ref[...], wd_ref[...])
    o_ref[...] = x + 0.5 * _rms(out, post_ref[...])


def _ffn_call(x2d, layer, pre_g, w_gu, w_down, post_g, tm):
    n_tiles = x2d.shape[0] // tm
    return pl.pallas_call(
        _ffn_kernel,
        out_shape=jax.ShapeDtypeStruct(x2d.shape, F32),
        grid=(n_tiles,),
        in_specs=[
            pl.BlockSpec((tm, D_MODEL), lambda i: (i, 0)),
            _layer_spec(layer, (1, D_MODEL)),
            _layer_spec(layer, (D_MODEL, 2 * D_FF)),
            _layer_spec(layer, (D_FF, D_MODEL)),
            _layer_spec(layer, (1, D_MODEL)),
        ],
        out_specs=pl.BlockSpec((tm, D_MODEL), lambda i: (i, 0)),
        scratch_shapes=[pltpu.VMEM((tm, D_FF), BF16)],
        compiler_params=pltpu.CompilerParams(
            dimension_semantics=("arbitrary",), vmem_limit_bytes=VMEM_LIMIT_BYTES),
        name="ffn",
    )(x2d, pre_g, w_gu, w_down, post_g)


def _lru_scan(a, u, h0, a_s, b_s):
    t = a.shape[0]
    row = lax.broadcasted_iota(jnp.int32, a.shape, 0) & (SUBLANES - 1)
    for k in (1, 2, 4):
        keep = row >= k
        a_sh = jnp.where(keep, pltpu.roll(a, k, axis=0), 1.0)
        u_sh = jnp.where(keep, pltpu.roll(u, k, axis=0), 0.0)
        u = a * u_sh + u
        a = a * a_sh
    a_s[...] = a
    b_s[...] = u

    def body(g, h):
        i = pl.multiple_of(g * SUBLANES, SUBLANES)
        hg = a_s[pl.ds(i, SUBLANES), :] * h + b_s[pl.ds(i, SUBLANES), :]
        b_s[pl.ds(i, SUBLANES), :] = hg
        return hg[SUBLANES - 1:SUBLANES, :]

    h_last = lax.fori_loop(0, t // SUBLANES, body, h0, unroll=True)
    return b_s[...], h_last


def _residues(halo, n_taps):
    return tuple(sorted({(halo - (n_taps - 1) + k) % SUBLANES for k in range(n_taps)} - {0}))


def _fill_shifted(ext_ref, sh_ref, residues):
    ext = ext_ref[...]
    n_rows = ext.shape[0]
    for n, s in enumerate(residues):
        sh_ref[n] = pltpu.roll(ext, n_rows - s, axis=0)


def _causal_dwconv(ext_ref, sh_ref, residues, w_ref, bias, halo, n_taps, r0, rows):
    acc = bias
    for k in range(n_taps):
        off = halo - (n_taps - 1) + k
        s = off % SUBLANES
        start = r0 + off - s
        if s == 0:
            tap = ext_ref[start:start + rows, :]
        else:
            tap = sh_ref[residues.index(s), start:start + rows, :]
        acc = acc + w_ref[k:k + 1, :] * tap
    return acc


def _gelu_tanh(x):
    return 0.5 * x * (1.0 + jnp.tanh(math.sqrt(2.0 / math.pi) * (x + 0.044715 * (x * x * x))))


def _mixer_kernel(layer, sinks_ref, x_ref, pre_ref, w_in_ref, lcw_ref, lcb_ref, wg_ref, bg_ref, lam_ref,
                  cw_ref, cb_ref, lng_ref, lnb_ref, gg_ref, w_out_ref, post_ref, o_ref,
                  lx_ext, lx_sh, glu_ext, glu_sh, kx, vx, h_ref, a_s, b_s, yb_ref, ycat_ref):
    t_idx = pl.program_id(1)
    tt = x_ref.shape[0]

    @pl.when(t_idx == 0)
    def _():
        lx_ext[0:LX_HALO, :] = jnp.zeros((LX_HALO, W_A), F32)
        glu_ext[0:GLU_HALO, :] = jnp.zeros((GLU_HALO, W_C), F32)
        kx[:, 0:BLK, :] = jnp.zeros((4, BLK, LANES), BF16)
        vx[:, 0:BLK, :] = jnp.zeros((4, BLK, LANES), BF16)
        h_ref[...] = jnp.zeros((1, W_A), F32)

    x = x_ref[...]
    hn = _rms(x, pre_ref[...]).astype(BF16)
    gg = gg_ref[...]
    proj = _dot(hn, w_in_ref[...])

    gate = proj[:, W_A:OFF_Q]
    lx_ext[LX_HALO:LX_HALO + tt, :] = proj[:, :W_A]
    lx_res = _residues(LX_HALO, LRU_CONV)
    _fill_shifted(lx_ext, lx_sh, lx_res)
    xc = _causal_dwconv(lx_ext, lx_sh, lx_res, lcw_ref, lcb_ref[...], LX_HALO, LRU_CONV, 0, tt)
    lx_ext[0:LX_HALO, :] = lx_ext[tt:tt + LX_HALO, :]
    gr = _dot(xc.astype(BF16), wg_ref[...]) + bg_ref[...]
    r = jax.nn.sigmoid(gr[:, :W_A])
    i_gate = jax.nn.sigmoid(gr[:, W_A:])
    z = -lam_ref[...]
    softplus = jnp.maximum(z, 0.0) + jnp.log1p(jnp.exp(-jnp.abs(z)))
    log_a = (-LRU_C) * r * softplus
    a = jnp.exp(log_a)
    th = jnp.tanh(log_a)
    u = jnp.sqrt(-2.0 * th / (1.0 - th)) * (i_gate * xc)
    h, h_last = _lru_scan(a, u, h_ref[...], a_s, b_s)
    h_ref[...] = h_last
    ycat_ref[:, 0:W_A] = _rms(_gelu_tanh(gate) * h, gg[:, :W_A]).astype(BF16)

    glu_ext[GLU_HALO:GLU_HALO + tt, :] = (proj[:, OFF_GLU:OFF_GLU + W_C]
                                          * jax.nn.sigmoid(proj[:, OFF_GLU + W_C:]))
    glu_res = _residues(GLU_HALO, C_CONV)
    _fill_shifted(glu_ext, glu_sh, glu_res)

    def conformer_rows(r0):
        yc = _causal_dwconv(glu_ext, glu_sh, glu_res, cw_ref, cb_ref[...], GLU_HALO, C_CONV, r0, CONV_ROWS)
        mu = jnp.mean(yc, axis=-1, keepdims=True)
        ycc = yc - mu
        var = jnp.mean(ycc * ycc, axis=-1, keepdims=True)
        yc = ycc * lax.rsqrt(var + LN_EPS) * lng_ref[...] + lnb_ref[...]
        y_c = yc * jax.nn.sigmoid(yc)
        ycat_ref[r0:r0 + CONV_ROWS, W_A + W_B:] = _rms(y_c, gg[:, W_A + W_B:]).astype(BF16)

    q = (proj[:, OFF_Q:OFF_K] * (1.0 / math.sqrt(HEAD_DIM))).astype(BF16)
    lane = lax.broadcasted_iota(jnp.int32, (tt, LANES), 1)
    lo = lane < HEAD_DIM
    for src, dst in ((proj[:, OFF_K:OFF_V], kx), (proj[:, OFF_V:OFF_GLU], vx)):
        rot = pltpu.roll(src, HEAD_DIM, axis=1)
        dst[0, BLK:BLK + tt, :] = jnp.where(lo, src, 0.0).astype(BF16)
        dst[1, BLK:BLK + tt, :] = jnp.where(lo, 0.0, rot).astype(BF16)
        dst[2, BLK:BLK + tt, :] = jnp.where(lo, rot, 0.0).astype(BF16)
        dst[3, BLK:BLK + tt, :] = jnp.where(lo, 0.0, src).astype(BF16)

    qi = lax.broadcasted_iota(jnp.int32, (2 * BLK, 2 * BLK), 0) & (BLK - 1)
    kj = lax.broadcasted_iota(jnp.int32, (2 * BLK, 2 * BLK), 1)
    rel = BLK + qi - kj
    band = (rel >= 0) & (rel < WINDOW)
    top = lax.broadcasted_iota(jnp.int32, (2 * BLK, 1), 0) < BLK
    for j in range(tt // BLK):
        if j == 0:
            mask = band & ((kj + (t_idx * tt - BLK)) >= 0)
        else:
            mask = band
        for g in range(N_KV_HEADS):
            p0 = 2 * g * LANES
            lhs = jnp.concatenate([q[j * BLK:(j + 1) * BLK, p0:p0 + LANES],
                                   q[j * BLK:(j + 1) * BLK, p0 + LANES:p0 + 2 * LANES]], axis=0)
            o = jnp.zeros((2 * BLK, LANES), F32)
            for e in range(2):
                kk = kx[2 * g + e, j * BLK:(j + 2) * BLK, :]
                vv = vx[2 * g + e, j * BLK:(j + 2) * BLK, :]
                s = lax.dot_general(lhs, kk, (((1,), (1,)), ((), ())), preferred_element_type=F32)
                s = jnp.where(mask, s, NEG_BIG)
                sink = jnp.where(top, sinks_ref[layer, 4 * g + e], sinks_ref[layer, 4 * g + 2 + e])
                m = jnp.maximum(jnp.max(s, axis=-1, keepdims=True), sink)
                p = jnp.exp(s - m)
                denom = jnp.sum(p, axis=-1, keepdims=True) + jnp.exp(sink - m)
                o = o + _dot(p.astype(BF16), vv) * (1.0 / denom)
            yb_ref[j * BLK:(j + 1) * BLK, p0:p0 + LANES] = o[:BLK]
            yb_ref[j * BLK:(j + 1) * BLK, p0 + LANES:p0 + 2 * LANES] = o[BLK:]
        for r0 in range(j * BLK, (j + 1) * BLK, CONV_ROWS):
            conformer_rows(r0)
    kx[:, 0:BLK, :] = kx[:, tt:tt + BLK, :]
    vx[:, 0:BLK, :] = vx[:, tt:tt + BLK, :]
    glu_ext[0:GLU_HALO, :] = glu_ext[tt:tt + GLU_HALO, :]
    ycat_ref[:, W_A:W_A + W_B] = _rms(yb_ref[...], gg[:, W_A:W_A + W_B]).astype(BF16)

    o_ref[...] = x + _rms(_dot(ycat_ref[...], w_out_ref[...]), post_ref[...])


def _mixer_call(x, layer, sinks, pre_g, w_in, lcw, lcb, wg, bg, lam, cw, cb, lng, lnb, gg, w_out, post_g, tt):
    bsz, s, _ = x.shape
    vec = lambda n: _layer_spec(layer, (1, n))
    return pl.pallas_call(
        functools.partial(_mixer_kernel, layer),
        out_shape=jax.ShapeDtypeStruct(x.shape, F32),
        grid=(bsz, s // tt),
        in_specs=[
            pl.BlockSpec(memory_space=pltpu.SMEM),
            pl.BlockSpec((None, tt, D_MODEL), lambda b, t: (b, t, 0)),
            vec(D_MODEL),
            _layer_spec(layer, (D_MODEL, D_IN_PROJ)),
            _layer_spec(layer, (LRU_CONV, W_A)),
            vec(W_A),
            _layer_spec(layer, (W_A, 2 * W_A)),
            vec(2 * W_A),
            vec(W_A),
            _layer_spec(layer, (C_CONV, W_C)),
            vec(W_C),
            vec(W_C),
            vec(W_C),
            vec(D_MIX),
            _layer_spec(layer, (D_MIX, D_MODEL)),
            vec(D_MODEL),
        ],
        out_specs=pl.BlockSpec((None, tt, D_MODEL), lambda b, t: (b, t, 0)),
        scratch_shapes=[
            pltpu.VMEM((LX_HALO + tt, W_A), F32),
            pltpu.VMEM((len(_residues(LX_HALO, LRU_CONV)), LX_HALO + tt, W_A), F32),
            pltpu.VMEM((GLU_HALO + tt, W_C), F32),
            pltpu.VMEM((len(_residues(GLU_HALO, C_CONV)), GLU_HALO + tt, W_C), F32),
            pltpu.VMEM((4, BLK + tt, LANES), BF16),
            pltpu.VMEM((4, BLK + tt, LANES), BF16),
            pltpu.VMEM((1, W_A), F32),
            pltpu.VMEM((tt, W_A), F32),
            pltpu.VMEM((tt, W_A), F32),
            pltpu.VMEM((tt, W_B), F32),
            pltpu.VMEM((tt, D_MIX), BF16),
        ],
        compiler_params=pltpu.CompilerParams(
            dimension_semantics=("arbitrary", "arbitrary"), vmem_limit_bytes=VMEM_LIMIT_BYTES),
        name="mixer",
    )(sinks, x, pre_g, w_in, lcw, lcb, wg, bg, lam, cw, cb, lng, lnb, gg, w_out, post_g)


def _block_diag(w):
    eye = jnp.eye(A_BLOCKS, dtype=w.dtype)
    return jnp.einsum("lhij,hg->lhigj", w, eye).reshape(w.shape[0], W_A, W_A)


def _pick_tile(n, target):
    t = min(n, target)
    while n % t:
        t //= 2
    return t


def kernel(x, ffn1_pre_g, ffn1_w_gu, ffn1_w_down, ffn1_post_g, mix_pre_g, w_in, lru_conv_w, lru_conv_b, lru_w_a, lru_b_a, lru_w_x, lru_b_x, lru_lambda, attn_sinks, conv_w, conv_b, conv_ln_g, conv_ln_b, group_g, w_out, mix_post_g, ffn2_pre_g, ffn2_w_gu, ffn2_w_down, ffn2_post_g):
    bsz, s, d = x.shape
    depth = w_in.shape[0]
    tm = _pick_tile(bsz * s, 512)
    tt = _pick_tile(s, 512)
    rows = lambda v: v.reshape(depth, 1, -1)
    bf = lambda w: w.astype(BF16)
    ffn1 = (rows(ffn1_pre_g), bf(ffn1_w_gu), bf(ffn1_w_down), rows(ffn1_post_g))
    ffn2 = (rows(ffn2_pre_g), bf(ffn2_w_gu), bf(ffn2_w_down), rows(ffn2_post_g))
    w_gates = bf(jnp.concatenate([_block_diag(lru_w_a), _block_diag(lru_w_x)], axis=2))
    b_gates = rows(jnp.concatenate([lru_b_a, lru_b_x], axis=1))
    mix = (attn_sinks, rows(mix_pre_g), bf(w_in), lru_conv_w, rows(lru_conv_b), w_gates, b_gates,
           rows(lru_lambda), conv_w, rows(conv_b), rows(conv_ln_g), rows(conv_ln_b), rows(group_g),
           bf(w_out), rows(mix_post_g))
    x2d = x.reshape(bsz * s, d)
    for l in range(depth):
        x2d = _ffn_call(x2d, l, *ffn1, tm)
        x2d = _mixer_call(x2d.reshape(bsz, s, d), l, *mix, tt).reshape(bsz * s, d)
        x2d = _ffn_call(x2d, l, *ffn2, tm)
    return x2d.reshape(bsz, s, d)
```

```python
import functools
import math

import jax
import jax.numpy as jnp
from jax import lax
from jax.experimental import pallas as pl
from jax.experimental.pallas import tpu as pltpu

D_MODEL = 1024
D_FF = 2816
HEAD_DIM = 64
W_A = 256
A_BLOCKS = 4
A_BLOCK_W = W_A // A_BLOCKS
LRU_C = 8.0
LRU_CONV = 4
N_Q_HEADS = 8
N_KV_HEADS = 2
W_B = N_Q_HEADS * HEAD_DIM
WINDOW = 128
BLK = 128
W_C = 256
C_CONV = 31
D_MIX = W_A + W_B + W_C
OFF_Q = 2 * W_A
OFF_K = OFF_Q + W_B
OFF_V = OFF_K + N_KV_HEADS * HEAD_DIM
OFF_GLU = OFF_V + N_KV_HEADS * HEAD_DIM
D_IN_PROJ = OFF_GLU + 2 * W_C
NORM_EPS = 1e-6
LN_EPS = 1e-5
NEG_BIG = -1e30
LOG2_E = math.log2(math.e)

SUBLANES = 8
LANES = 128
MXU_DIM = 256

FF_CHUNK = MXU_DIM
N_FF_CHUNKS = D_FF // FF_CHUNK
LX_HALO = SUBLANES
GLU_HALO = 4 * SUBLANES
CONV_ROWS = 64
VMEM_LIMIT_BYTES = 52 * 1024 * 1024

F32 = jnp.float32
BF16 = jnp.bfloat16


def _unit_rms(x):
    return x * lax.rsqrt(jnp.mean(x * x, axis=-1, keepdims=True) + NORM_EPS)


def _dot(a, b):
    return jnp.dot(a, b, preferred_element_type=F32)


def _layer_spec(layer, shape):
    idx = (layer,) + (0,) * len(shape)
    return pl.BlockSpec((None,) + tuple(shape), lambda *_: idx, pipeline_mode=pl.Buffered(1))


def _ffn_kernel(x_ref, wgu_ref, wd_ref, post_ref, o_ref, act_ref):
    x = x_ref[...]
    xn = _unit_rms(x).astype(BF16)
    for j in range(N_FF_CHUNKS):
        g = _dot(xn, wgu_ref[:, j * FF_CHUNK:(j + 1) * FF_CHUNK])
        u = _dot(xn, wgu_ref[:, D_FF + j * FF_CHUNK:D_FF + (j + 1) * FF_CHUNK])
        act_ref[:, j * FF_CHUNK:(j + 1) * FF_CHUNK] = (g * jax.nn.sigmoid(g) * u).astype(BF16)
    out = _dot(act_ref[...], wd_ref[...])
    o_ref[...] = x + _unit_rms(out) * (0.5 * post_ref[...])


def _ffn_call(x2d, layer, w_gu, w_down, post_g, tm):
    n_tiles = x2d.shape[0] // tm
    return pl.pallas_call(
        _ffn_kernel,
        out_shape=jax.ShapeDtypeStruct(x2d.shape, F32),
        grid=(n_tiles,),
        in_specs=[
            pl.BlockSpec((tm, D_MODEL), lambda i: (i, 0)),
            _layer_spec(layer, (D_MODEL, 2 * D_FF)),
            _layer_spec(layer, (D_FF, D_MODEL)),
            _layer_spec(layer, (1, D_MODEL)),
        ],
        out_specs=pl.BlockSpec((tm, D_MODEL), lambda i: (i, 0)),
        scratch_shapes=[pltpu.VMEM((tm, D_FF), BF16)],
        compiler_params=pltpu.CompilerParams(
            dimension_semantics=("arbitrary",), vmem_limit_bytes=VMEM_LIMIT_BYTES),
        name="ffn",
    )(x2d, w_gu, w_down, post_g)


def _lru_scan(a, u, h0, a_s, b_s):
    t = a.shape[0]
    row = lax.broadcasted_iota(jnp.int32, a.shape, 0) & (SUBLANES - 1)
    for k in (1, 2, 4):
        keep = row >= k
        a_sh = jnp.where(keep, pltpu.roll(a, k, axis=0), 1.0)
        u_sh = jnp.where(keep, pltpu.roll(u, k, axis=0), 0.0)
        u = a * u_sh + u
        a = a * a_sh
    a_s[...] = a
    b_s[...] = u

    def body(g, h):
        i = pl.multiple_of(g * SUBLANES, SUBLANES)
        hg = a_s[pl.ds(i, SUBLANES), :] * h + b_s[pl.ds(i, SUBLANES), :]
        b_s[pl.ds(i, SUBLANES), :] = hg
        return hg[SUBLANES - 1:SUBLANES, :]

    h_last = lax.fori_loop(0, t // SUBLANES, body, h0, unroll=True)
    return b_s[...], h_last


def _residues(halo, n_taps):
    return tuple(sorted({(halo - (n_taps - 1) + k) % SUBLANES for k in range(n_taps)} - {0}))


def _fill_shifted(ext_ref, sh_ref, residues):
    ext = ext_ref[...]
    n_rows = ext.shape[0]
    for n, s in enumerate(residues):
        sh_ref[n] = pltpu.roll(ext, n_rows - s, axis=0)


def _causal_dwconv(ext_ref, sh_ref, residues, w_ref, bias, halo, n_taps, r0, rows):
    acc = bias
    for k in range(n_taps):
        off = halo - (n_taps - 1) + k
        s = off % SUBLANES
        start = r0 + off - s
        if s == 0:
            tap = ext_ref[start:start + rows, :]
        else:
            tap = sh_ref[residues.index(s), start:start + rows, :]
        acc = acc + pltpu.repeat(w_ref[k], rows // SUBLANES, axis=0) * tap
    return acc


def _gelu_tanh(x):
    return 0.5 * x * (1.0 + jnp.tanh(math.sqrt(2.0 / math.pi) * (x + 0.044715 * (x * x * x))))


def _mixer_kernel(layer, sinks_ref, x_ref, w_in_ref, lcw_ref, lcb_ref, wg_ref, bg_ref, lam_ref,
                  cw_ref, cb_ref, lng_ref, lnb_ref, w_out_ref, post_ref, o_ref,
                  lx_ext, lx_sh, glu_ext, glu_sh, kx, vx, h_ref, a_s, b_s, yb_ref, ycat_ref):
    t_idx = pl.program_id(1)
    tt = x_ref.shape[0]

    @pl.when(t_idx == 0)
    def _():
        lx_ext[0:LX_HALO, :] = jnp.zeros((LX_HALO, W_A), F32)
        glu_ext[0:GLU_HALO, :] = jnp.zeros((GLU_HALO, W_C), F32)
        kx[:, 0:BLK, :] = jnp.zeros((4, BLK, LANES), BF16)
        vx[:, 0:BLK, :] = jnp.zeros((4, BLK, LANES), BF16)
        h_ref[...] = jnp.zeros((1, W_A), F32)

    x = x_ref[...]
    hn = _unit_rms(x).astype(BF16)
    proj = _dot(hn, w_in_ref[...])

    gate = proj[:, W_A:OFF_Q]
    lx_ext[LX_HALO:LX_HALO + tt, :] = proj[:, :W_A]
    lx_res = _residues(LX_HALO, LRU_CONV)
    _fill_shifted(lx_ext, lx_sh, lx_res)
    xc = _causal_dwconv(lx_ext, lx_sh, lx_res, lcw_ref, lcb_ref[...], LX_HALO, LRU_CONV, 0, tt)
    lx_ext[0:LX_HALO, :] = lx_ext[tt:tt + LX_HALO, :]
    gr = _dot(xc.astype(BF16), wg_ref[...]) + bg_ref[...]
    r = jax.nn.sigmoid(gr[:, :W_A])
    i_gate = jax.nn.sigmoid(gr[:, W_A:])
    z = -lam_ref[...]
    softplus = jnp.maximum(z, 0.0) + jnp.log1p(jnp.exp(-jnp.abs(z)))
    log_a = (-LRU_C) * r * softplus
    a = jnp.exp(log_a)
    th = jnp.tanh(log_a)
    u = jnp.sqrt(-2.0 * th / (1.0 - th)) * (i_gate * xc)
    h, h_last = _lru_scan(a, u, h_ref[...], a_s, b_s)
    h_ref[...] = h_last
    ycat_ref[:, 0:W_A] = _unit_rms(_gelu_tanh(gate) * h).astype(BF16)

    glu_ext[GLU_HALO:GLU_HALO + tt, :] = (proj[:, OFF_GLU:OFF_GLU + W_C]
                                          * jax.nn.sigmoid(proj[:, OFF_GLU + W_C:]))
    glu_res = _residues(GLU_HALO, C_CONV)
    _fill_shifted(glu_ext, glu_sh, glu_res)

    def conformer_rows(r0):
        yc = _causal_dwconv(glu_ext, glu_sh, glu_res, cw_ref, cb_ref[...], GLU_HALO, C_CONV, r0, CONV_ROWS)
        mu = jnp.mean(yc, axis=-1, keepdims=True)
        ycc = yc - mu
        var = jnp.mean(ycc * ycc, axis=-1, keepdims=True)
        yc = ycc * lax.rsqrt(var + LN_EPS) * lng_ref[...] + lnb_ref[...]
        y_c = yc * jax.nn.sigmoid(yc)
        ycat_ref[r0:r0 + CONV_ROWS, W_A + W_B:] = _unit_rms(y_c).astype(BF16)

    q = (proj[:, OFF_Q:OFF_K] * (LOG2_E / math.sqrt(HEAD_DIM))).astype(BF16)
    lane = lax.broadcasted_iota(jnp.int32, (tt, LANES), 1)
    lo = lane < HEAD_DIM
    for src, dst in ((proj[:, OFF_K:OFF_V], kx), (proj[:, OFF_V:OFF_GLU], vx)):
        rot = pltpu.roll(src, HEAD_DIM, axis=1)
        dst[0, BLK:BLK + tt, :] = jnp.where(lo, src, 0.0).astype(BF16)
        dst[1, BLK:BLK + tt, :] = jnp.where(lo, 0.0, rot).astype(BF16)
        dst[2, BLK:BLK + tt, :] = jnp.where(lo, rot, 0.0).astype(BF16)
        dst[3, BLK:BLK + tt, :] = jnp.where(lo, 0.0, src).astype(BF16)

    qi = lax.broadcasted_iota(jnp.int32, (2 * BLK, 2 * BLK), 0) & (BLK - 1)
    kj = lax.broadcasted_iota(jnp.int32, (2 * BLK, 2 * BLK), 1)
    rel = BLK + qi - kj
    band = (rel >= 0) & (rel < WINDOW)
    top = lax.broadcasted_iota(jnp.int32, (2 * BLK, 1), 0) < BLK
    for j in range(tt // BLK):
        if j == 0:
            mask = band & ((kj + (t_idx * tt - BLK)) >= 0)
        else:
            mask = band
        for g in range(N_KV_HEADS):
            p0 = 2 * g * LANES
            lhs = jnp.concatenate([q[j * BLK:(j + 1) * BLK, p0:p0 + LANES],
                                   q[j * BLK:(j + 1) * BLK, p0 + LANES:p0 + 2 * LANES]], axis=0)
            o = jnp.zeros((2 * BLK, LANES), F32)
            for e in range(2):
                kk = kx[2 * g + e, j * BLK:(j + 2) * BLK, :]
                vv = vx[2 * g + e, j * BLK:(j + 2) * BLK, :]
                s = lax.dot_general(lhs, kk, (((1,), (1,)), ((), ())), preferred_element_type=F32)
                s = jnp.where(mask, s, NEG_BIG)
                sink = jnp.where(top, sinks_ref[layer, 4 * g + e], sinks_ref[layer, 4 * g + 2 + e]) * LOG2_E
                m = jnp.max(s, axis=-1, keepdims=True)
                p = jnp.exp2(s - m)
                denom = jnp.sum(p, axis=-1, keepdims=True) + jnp.exp2(sink - m)
                o = o + _dot(p.astype(BF16), vv) * (1.0 / denom)
            yb_ref[j * BLK:(j + 1) * BLK, p0:p0 + LANES] = o[:BLK]
            yb_ref[j * BLK:(j + 1) * BLK, p0 + LANES:p0 + 2 * LANES] = o[BLK:]
        for r0 in range(j * BLK, (j + 1) * BLK, CONV_ROWS):
            conformer_rows(r0)
    kx[:, 0:BLK, :] = kx[:, tt:tt + BLK, :]
    vx[:, 0:BLK, :] = vx[:, tt:tt + BLK, :]
    glu_ext[0:GLU_HALO, :] = glu_ext[tt:tt + GLU_HALO, :]
    ycat_ref[:, W_A:W_A + W_B] = _unit_rms(yb_ref[...]).astype(BF16)

    o_ref[...] = x + _unit_rms(_dot(ycat_ref[...], w_out_ref[...])) * post_ref[...]


def _mixer_call(x, layer, sinks, w_in, lcw, lcb, wg, bg, lam, cw, cb, lng, lnb, w_out, post_g, tt):
    bsz, s, _ = x.shape
    vec = lambda n: _layer_spec(layer, (1, n))
    return pl.pallas_call(
        functools.partial(_mixer_kernel, layer),
        out_shape=jax.ShapeDtypeStruct(x.shape, F32),
        grid=(bsz, s // tt),
        in_specs=[
            pl.BlockSpec(memory_space=pltpu.SMEM),
            pl.BlockSpec((None, tt, D_MODEL), lambda b, t: (b, t, 0)),
            _layer_spec(layer, (D_MODEL, D_IN_PROJ)),
            _layer_spec(layer, (LRU_CONV, SUBLANES, W_A)),
            vec(W_A),
            _layer_spec(layer, (W_A, 2 * W_A)),
            vec(2 * W_A),
            vec(W_A),
            _layer_spec(layer, (C_CONV, SUBLANES, W_C)),
            vec(W_C),
            vec(W_C),
            vec(W_C),
            _layer_spec(layer, (D_MIX, D_MODEL)),
            vec(D_MODEL),
        ],
        out_specs=pl.BlockSpec((None, tt, D_MODEL), lambda b, t: (b, t, 0)),
        scratch_shapes=[
            pltpu.VMEM((LX_HALO + tt, W_A), F32),
            pltpu.VMEM((len(_residues(LX_HALO, LRU_CONV)), LX_HALO + tt, W_A), F32),
            pltpu.VMEM((GLU_HALO + tt, W_C), F32),
            pltpu.VMEM((len(_residues(GLU_HALO, C_CONV)), GLU_HALO + tt, W_C), F32),
            pltpu.VMEM((4, BLK + tt, LANES), BF16),
            pltpu.VMEM((4, BLK + tt, LANES), BF16),
            pltpu.VMEM((1, W_A), F32),
            pltpu.VMEM((tt, W_A), F32),
            pltpu.VMEM((tt, W_A), F32),
            pltpu.VMEM((tt, W_B), F32),
            pltpu.VMEM((tt, D_MIX), BF16),
        ],
        compiler_params=pltpu.CompilerParams(
            dimension_semantics=("arbitrary", "arbitrary"), vmem_limit_bytes=VMEM_LIMIT_BYTES),
        name="mixer",
    )(sinks, x, w_in, lcw, lcb, wg, bg, lam, cw, cb, lng, lnb, w_out, post_g)


def _block_diag(w):
    eye = jnp.eye(A_BLOCKS, dtype=w.dtype)
    return jnp.einsum("lhij,hg->lhigj", w, eye).reshape(w.shape[0], W_A, W_A)


def _pick_tile(n, target):
    t = min(n, target)
    while n % t:
        t //= 2
    return t


def kernel(x, ffn1_pre_g, ffn1_w_gu, ffn1_w_down, ffn1_post_g, mix_pre_g, w_in, lru_conv_w, lru_conv_b, lru_w_a, lru_b_a, lru_w_x, lru_b_x, lru_lambda, attn_sinks, conv_w, conv_b, conv_ln_g, conv_ln_b, group_g, w_out, mix_post_g, ffn2_pre_g, ffn2_w_gu, ffn2_w_down, ffn2_post_g):
    bsz, s, d = x.shape
    depth = w_in.shape[0]
    tm = _pick_tile(bsz * s, 512)
    tt = _pick_tile(s, 512)
    rows = lambda v: v.reshape(depth, 1, -1)
    bf = lambda w: w.astype(BF16)
    scaled = lambda g, w: bf(g[:, :, None] * w)
    taps = lambda w: jnp.broadcast_to(w[:, :, None, :], w.shape[:2] + (SUBLANES,) + w.shape[2:])
    ffn1 = (scaled(ffn1_pre_g, ffn1_w_gu), bf(ffn1_w_down), rows(ffn1_post_g))
    ffn2 = (scaled(ffn2_pre_g, ffn2_w_gu), bf(ffn2_w_down), rows(ffn2_post_g))
    w_gates = bf(jnp.concatenate([_block_diag(lru_w_a), _block_diag(lru_w_x)], axis=2))
    b_gates = rows(jnp.concatenate([lru_b_a, lru_b_x], axis=1))
    mix = (attn_sinks, scaled(mix_pre_g, w_in), taps(lru_conv_w), rows(lru_conv_b), w_gates, b_gates,
           rows(lru_lambda), taps(conv_w), rows(conv_b), rows(conv_ln_g), rows(conv_ln_b),
           scaled(group_g, w_out), rows(mix_post_g))
    x2d = x.reshape(bsz * s, d)
    for l in range(depth):
        x2d = _ffn_call(x2d, l, *ffn1, tm)
        x2d = _mixer_call(x2d.reshape(bsz, s, d), l, *mix, tt).reshape(bsz * s, d)
        x2d = _ffn_call(x2d, l, *ffn2, tm)
    return x2d.reshape(bsz, s, d)
```

```python
import functools
import math

import jax
import jax.numpy as jnp
from jax import lax
from jax.experimental import pallas as pl
from jax.experimental.pallas import tpu as pltpu

D_MODEL = 1024
D_FF = 2816
HEAD_DIM = 64
W_A = 256
A_BLOCKS = 4
A_BLOCK_W = W_A // A_BLOCKS
LRU_C = 8.0
LRU_CONV = 4
N_Q_HEADS = 8
N_KV_HEADS = 2
W_B = N_Q_HEADS * HEAD_DIM
WINDOW = 128
BLK = 128
W_C = 256
C_CONV = 31
D_MIX = W_A + W_B + W_C
OFF_Q = 2 * W_A
OFF_K = OFF_Q + W_B
OFF_V = OFF_K + N_KV_HEADS * HEAD_DIM
OFF_GLU = OFF_V + N_KV_HEADS * HEAD_DIM
D_IN_PROJ = OFF_GLU + 2 * W_C
NORM_EPS = 1e-6
LN_EPS = 1e-5
NEG_BIG = -1e30
LOG2_E = math.log2(math.e)

SUBLANES = 8
LANES = 128
MXU_DIM = 256

FF_CHUNK = MXU_DIM
N_FF_CHUNKS = D_FF // FF_CHUNK
LX_HALO = SUBLANES
GLU_HALO = 4 * SUBLANES
CONV_ROWS = 64
VMEM_LIMIT_BYTES = 52 * 1024 * 1024

F32 = jnp.float32
BF16 = jnp.bfloat16


def _unit_rms(x):
    return x * lax.rsqrt(jnp.mean(x * x, axis=-1, keepdims=True) + NORM_EPS)


def _dot(a, b):
    return jnp.dot(a, b, preferred_element_type=F32)


def _layer_spec(layer, shape):
    idx = (layer,) + (0,) * len(shape)
    return pl.BlockSpec((None,) + tuple(shape), lambda *_: idx, pipeline_mode=pl.Buffered(1))


def _ffn_kernel(x_ref, wgu_ref, wd_ref, post_ref, o_ref, act_ref):
    x = x_ref[...]
    xn = _unit_rms(x).astype(BF16)
    for j in range(N_FF_CHUNKS):
        g = _dot(xn, wgu_ref[:, j * FF_CHUNK:(j + 1) * FF_CHUNK])
        u = _dot(xn, wgu_ref[:, D_FF + j * FF_CHUNK:D_FF + (j + 1) * FF_CHUNK])
        act_ref[:, j * FF_CHUNK:(j + 1) * FF_CHUNK] = (g * jax.nn.sigmoid(g) * u).astype(BF16)
    out = _dot(act_ref[...], wd_ref[...])
    o_ref[...] = x + _unit_rms(out) * (0.5 * post_ref[...])


def _ffn_call(x2d, layer, w_gu, w_down, post_g, tm):
    n_tiles = x2d.shape[0] // tm
    return pl.pallas_call(
        _ffn_kernel,
        out_shape=jax.ShapeDtypeStruct(x2d.shape, F32),
        grid=(n_tiles,),
        in_specs=[
            pl.BlockSpec((tm, D_MODEL), lambda i: (i, 0)),
            _layer_spec(layer, (D_MODEL, 2 * D_FF)),
            _layer_spec(layer, (D_FF, D_MODEL)),
            _layer_spec(layer, (1, D_MODEL)),
        ],
        out_specs=pl.BlockSpec((tm, D_MODEL), lambda i: (i, 0)),
        scratch_shapes=[pltpu.VMEM((tm, D_FF), BF16)],
        compiler_params=pltpu.CompilerParams(
            dimension_semantics=("arbitrary",), vmem_limit_bytes=VMEM_LIMIT_BYTES),
        name="ffn",
    )(x2d, w_gu, w_down, post_g)


def _lru_scan(a, u, h0, a_s, b_s):
    t = a.shape[0]
    row = lax.broadcasted_iota(jnp.int32, a.shape, 0) & (SUBLANES - 1)
    for k in (1, 2, 4):
        keep = row >= k
        a_sh = jnp.where(keep, pltpu.roll(a, k, axis=0), 1.0)
        u_sh = jnp.where(keep, pltpu.roll(u, k, axis=0), 0.0)
        u = a * u_sh + u
        a = a * a_sh
    a_s[...] = a
    b_s[...] = u

    def body(g, h):
        i = pl.multiple_of(g * SUBLANES, SUBLANES)
        hg = a_s[pl.ds(i, SUBLANES), :] * h + b_s[pl.ds(i, SUBLANES), :]
        b_s[pl.ds(i, SUBLANES), :] = hg
        return hg[SUBLANES - 1:SUBLANES, :]

    h_last = lax.fori_loop(0, t // SUBLANES, body, h0, unroll=True)
    return b_s[...], h_last


def _residues(halo, n_taps):
    return tuple(sorted({(halo - (n_taps - 1) + k) % SUBLANES for k in range(n_taps)} - {0}))


def _fill_shifted(ext_ref, sh_ref, residues):
    ext = ext_ref[...]
    n_rows = ext.shape[0]
    for n, s in enumerate(residues):
        sh_ref[n] = pltpu.roll(ext, n_rows - s, axis=0)


def _causal_dwconv(ext_ref, sh_ref, residues, w_ref, bias, halo, n_taps, r0, rows):
    acc = bias
    for k in range(n_taps):
        off = halo - (n_taps - 1) + k
        s = off % SUBLANES
        start = r0 + off - s
        if s == 0:
            tap = ext_ref[start:start + rows, :]
        else:
            tap = sh_ref[residues.index(s), start:start + rows, :]
        acc = acc + jnp.concatenate([w_ref[k]] * (rows // SUBLANES), axis=0) * tap
    return acc


def _gelu_tanh(x):
    return 0.5 * x * (1.0 + jnp.tanh(math.sqrt(2.0 / math.pi) * (x + 0.044715 * (x * x * x))))


def _mixer_kernel(layer, sinks_ref, x_ref, w_in_ref, lcw_ref, lcb_ref, wg_ref, bg_ref, lam_ref,
                  cw_ref, cb_ref, lng_ref, lnb_ref, w_out_ref, post_ref, o_ref,
                  lx_ext, lx_sh, glu_ext, glu_sh, kx, vx, h_ref, a_s, b_s, yb_ref, ycat_ref):
    t_idx = pl.program_id(1)
    tt = x_ref.shape[0]

    @pl.when(t_idx == 0)
    def _():
        lx_ext[0:LX_HALO, :] = jnp.zeros((LX_HALO, W_A), F32)
        glu_ext[0:GLU_HALO, :] = jnp.zeros((GLU_HALO, W_C), F32)
        kx[:, 0:BLK, :] = jnp.zeros((4, BLK, LANES), BF16)
        vx[:, 0:BLK, :] = jnp.zeros((4, BLK, LANES), BF16)
        h_ref[...] = jnp.zeros((1, W_A), F32)

    x = x_ref[...]
    hn = _unit_rms(x).astype(BF16)
    proj = _dot(hn, w_in_ref[...])

    gate = proj[:, W_A:OFF_Q]
    lx_ext[LX_HALO:LX_HALO + tt, :] = proj[:, :W_A]
    lx_res = _residues(LX_HALO, LRU_CONV)
    _fill_shifted(lx_ext, lx_sh, lx_res)
    xc = _causal_dwconv(lx_ext, lx_sh, lx_res, lcw_ref, lcb_ref[...], LX_HALO, LRU_CONV, 0, tt)
    lx_ext[0:LX_HALO, :] = lx_ext[tt:tt + LX_HALO, :]
    gr = _dot(xc.astype(BF16), wg_ref[...]) + bg_ref[...]
    r = jax.nn.sigmoid(gr[:, :W_A])
    i_gate = jax.nn.sigmoid(gr[:, W_A:])
    z = -lam_ref[...]
    softplus = jnp.maximum(z, 0.0) + jnp.log1p(jnp.exp(-jnp.abs(z)))
    log_a = (-LRU_C) * r * softplus
    a = jnp.exp(log_a)
    th = jnp.tanh(log_a)
    u = jnp.sqrt(-2.0 * th / (1.0 - th)) * (i_gate * xc)
    h, h_last = _lru_scan(a, u, h_ref[...], a_s, b_s)
    h_ref[...] = h_last
    ycat_ref[:, 0:W_A] = _unit_rms(_gelu_tanh(gate) * h).astype(BF16)

    glu_ext[GLU_HALO:GLU_HALO + tt, :] = (proj[:, OFF_GLU:OFF_GLU + W_C]
                                          * jax.nn.sigmoid(proj[:, OFF_GLU + W_C:]))
    glu_res = _residues(GLU_HALO, C_CONV)
    _fill_shifted(glu_ext, glu_sh, glu_res)

    def conformer_rows(r0):
        yc = _causal_dwconv(glu_ext, glu_sh, glu_res, cw_ref, cb_ref[...], GLU_HALO, C_CONV, r0, CONV_ROWS)
        mu = jnp.mean(yc, axis=-1, keepdims=True)
        ycc = yc - mu
        var = jnp.mean(ycc * ycc, axis=-1, keepdims=True)
        yc = ycc * lax.rsqrt(var + LN_EPS) * lng_ref[...] + lnb_ref[...]
        y_c = yc * jax.nn.sigmoid(yc)
        ycat_ref[r0:r0 + CONV_ROWS, W_A + W_B:] = _unit_rms(y_c).astype(BF16)

    q = (proj[:, OFF_Q:OFF_K] * (LOG2_E / math.sqrt(HEAD_DIM))).astype(BF16)
    lane = lax.broadcasted_iota(jnp.int32, (tt, LANES), 1)
    lo = lane < HEAD_DIM
    for src, dst in ((proj[:, OFF_K:OFF_V], kx), (proj[:, OFF_V:OFF_GLU], vx)):
        rot = pltpu.roll(src, HEAD_DIM, axis=1)
        dst[0, BLK:BLK + tt, :] = jnp.where(lo, src, 0.0).astype(BF16)
        dst[1, BLK:BLK + tt, :] = jnp.where(lo, 0.0, rot).astype(BF16)
        dst[2, BLK:BLK + tt, :] = jnp.where(lo, rot, 0.0).astype(BF16)
        dst[3, BLK:BLK + tt, :] = jnp.where(lo, 0.0, src).astype(BF16)

    qi = lax.broadcasted_iota(jnp.int32, (2 * BLK, 2 * BLK), 0) & (BLK - 1)
    kj = lax.broadcasted_iota(jnp.int32, (2 * BLK, 2 * BLK), 1)
    rel = BLK + qi - kj
    band = (rel >= 0) & (rel < WINDOW)
    top = lax.broadcasted_iota(jnp.int32, (2 * BLK, 1), 0) < BLK
    for j in range(tt // BLK):
        if j == 0:
            mask = band & ((kj + (t_idx * tt - BLK)) >= 0)
        else:
            mask = band
        for g in range(N_KV_HEADS):
            p0 = 2 * g * LANES
            lhs = jnp.concatenate([q[j * BLK:(j + 1) * BLK, p0:p0 + LANES],
                                   q[j * BLK:(j + 1) * BLK, p0 + LANES:p0 + 2 * LANES]], axis=0)
            o = jnp.zeros((2 * BLK, LANES), F32)
            for e in range(2):
                kk = kx[2 * g + e, j * BLK:(j + 2) * BLK, :]
                vv = vx[2 * g + e, j * BLK:(j + 2) * BLK, :]
                s = lax.dot_general(lhs, kk, (((1,), (1,)), ((), ())), preferred_element_type=F32)
                s = jnp.where(mask, s, NEG_BIG)
                sink = jnp.where(top, sinks_ref[layer, 4 * g + e], sinks_ref[layer, 4 * g + 2 + e]) * LOG2_E
                m = jnp.max(s, axis=-1, keepdims=True)
                p = jnp.exp2(s - m)
                denom = jnp.sum(p, axis=-1, keepdims=True) + jnp.exp2(sink - m)
                o = o + _dot(p.astype(BF16), vv) * (1.0 / denom)
            yb_ref[j * BLK:(j + 1) * BLK, p0:p0 + LANES] = o[:BLK]
            yb_ref[j * BLK:(j + 1) * BLK, p0 + LANES:p0 + 2 * LANES] = o[BLK:]
        for r0 in range(j * BLK, (j + 1) * BLK, CONV_ROWS):
            conformer_rows(r0)
    kx[:, 0:BLK, :] = kx[:, tt:tt + BLK, :]
    vx[:, 0:BLK, :] = vx[:, tt:tt + BLK, :]
    glu_ext[0:GLU_HALO, :] = glu_ext[tt:tt + GLU_HALO, :]
    ycat_ref[:, W_A:W_A + W_B] = _unit_rms(yb_ref[...]).astype(BF16)

    o_ref[...] = x + _unit_rms(_dot(ycat_ref[...], w_out_ref[...])) * post_ref[...]


def _mixer_call(x, layer, sinks, w_in, lcw, lcb, wg, bg, lam, cw, cb, lng, lnb, w_out, post_g, tt):
    bsz, s, _ = x.shape
    vec = lambda n: _layer_spec(layer, (1, n))
    return pl.pallas_call(
        functools.partial(_mixer_kernel, layer),
        out_shape=jax.ShapeDtypeStruct(x.shape, F32),
        grid=(bsz, s // tt),
        in_specs=[
            pl.BlockSpec(memory_space=pltpu.SMEM),
            pl.BlockSpec((None, tt, D_MODEL), lambda b, t: (b, t, 0)),
            _layer_spec(layer, (D_MODEL, D_IN_PROJ)),
            _layer_spec(layer, (LRU_CONV, SUBLANES, W_A)),
            vec(W_A),
            _layer_spec(layer, (W_A, 2 * W_A)),
            vec(2 * W_A),
            vec(W_A),
            _layer_spec(layer, (C_CONV, SUBLANES, W_C)),
            vec(W_C),
            vec(W_C),
            vec(W_C),
            _layer_spec(layer, (D_MIX, D_MODEL)),
            vec(D_MODEL),
        ],
        out_specs=pl.BlockSpec((None, tt, D_MODEL), lambda b, t: (b, t, 0)),
        scratch_shapes=[
            pltpu.VMEM((LX_HALO + tt, W_A), F32),
            pltpu.VMEM((len(_residues(LX_HALO, LRU_CONV)), LX_HALO + tt, W_A), F32),
            pltpu.VMEM((GLU_HALO + tt, W_C), F32),
            pltpu.VMEM((len(_residues(GLU_HALO, C_CONV)), GLU_HALO + tt, W_C), F32),
            pltpu.VMEM((4, BLK + tt, LANES), BF16),
            pltpu.VMEM((4, BLK + tt, LANES), BF16),
            pltpu.VMEM((1, W_A), F32),
            pltpu.VMEM((tt, W_A), F32),
            pltpu.VMEM((tt, W_A), F32),
            pltpu.VMEM((tt, W_B), F32),
            pltpu.VMEM((tt, D_MIX), BF16),
        ],
        compiler_params=pltpu.CompilerParams(
            dimension_semantics=("arbitrary", "arbitrary"), vmem_limit_bytes=VMEM_LIMIT_BYTES),
        name="mixer",
    )(sinks, x, w_in, lcw, lcb, wg, bg, lam, cw, cb, lng, lnb, w_out, post_g)


def _block_diag(w):
    eye = jnp.eye(A_BLOCKS, dtype=w.dtype)
    return jnp.einsum("lhij,hg->lhigj", w, eye).reshape(w.shape[0], W_A, W_A)


def _pick_tile(n, target):
    t = min(n, target)
    while n % t:
        t //= 2
    return t


def kernel(x, ffn1_pre_g, ffn1_w_gu, ffn1_w_down, ffn1_post_g, mix_pre_g, w_in, lru_conv_w, lru_conv_b, lru_w_a, lru_b_a, lru_w_x, lru_b_x, lru_lambda, attn_sinks, conv_w, conv_b, conv_ln_g, conv_ln_b, group_g, w_out, mix_post_g, ffn2_pre_g, ffn2_w_gu, ffn2_w_down, ffn2_post_g):
    bsz, s, d = x.shape
    depth = w_in.shape[0]
    tm = _pick_tile(bsz * s, 1024)
    tt = _pick_tile(s, 1024)
    rows = lambda v: v.reshape(depth, 1, -1)
    bf = lambda w: w.astype(BF16)
    scaled = lambda g, w: bf(g[:, :, None] * w)
    taps = lambda w: jnp.broadcast_to(w[:, :, None, :], w.shape[:2] + (SUBLANES,) + w.shape[2:])
    ffn1 = (scaled(ffn1_pre_g, ffn1_w_gu), bf(ffn1_w_down), rows(ffn1_post_g))
    ffn2 = (scaled(ffn2_pre_g, ffn2_w_gu), bf(ffn2_w_down), rows(ffn2_post_g))
    w_gates = bf(jnp.concatenate([_block_diag(lru_w_a), _block_diag(lru_w_x)], axis=2))
    b_gates = rows(jnp.concatenate([lru_b_a, lru_b_x], axis=1))
    mix = (attn_sinks, scaled(mix_pre_g, w_in), taps(lru_conv_w), rows(lru_conv_b), w_gates, b_gates,
           rows(lru_lambda), taps(conv_w), rows(conv_b), rows(conv_ln_g), rows(conv_ln_b),
           scaled(group_g, w_out), rows(mix_post_g))
    x2d = x.reshape(bsz * s, d)
    for l in range(depth):
        x2d = _ffn_call(x2d, l, *ffn1, tm)
        x2d = _mixer_call(x2d.reshape(bsz, s, d), l, *mix, tt).reshape(bsz * s, d)
        x2d = _ffn_call(x2d, l, *ffn2, tm)
    return x2d.reshape(bsz, s, d)
```

```python
import functools
import math

import jax
import jax.numpy as jnp
from jax import lax
from jax.experimental import pallas as pl
from jax.experimental.pallas import tpu as pltpu

D_MODEL = 1024
D_FF = 2816
HEAD_DIM = 64
W_A = 256
A_BLOCKS = 4
A_BLOCK_W = W_A // A_BLOCKS
LRU_C = 8.0
LRU_CONV = 4
N_Q_HEADS = 8
N_KV_HEADS = 2
W_B = N_Q_HEADS * HEAD_DIM
WINDOW = 128
BLK = 128
W_C = 256
C_CONV = 31
D_MIX = W_A + W_B + W_C
OFF_Q = 2 * W_A
OFF_K = OFF_Q + W_B
OFF_V = OFF_K + N_KV_HEADS * HEAD_DIM
OFF_GLU = OFF_V + N_KV_HEADS * HEAD_DIM
D_IN_PROJ = OFF_GLU + 2 * W_C
NORM_EPS = 1e-6
LN_EPS = 1e-5
NEG_BIG = -1e30
LOG2_E = math.log2(math.e)

SUBLANES = 8
LANES = 128
MXU_DIM = 256

FF_CHUNK = MXU_DIM
N_FF_CHUNKS = D_FF // FF_CHUNK
LX_HALO = SUBLANES
GLU_HALO = 4 * SUBLANES
CONV_ROWS = 64
VMEM_LIMIT_BYTES = 52 * 1024 * 1024

F32 = jnp.float32
BF16 = jnp.bfloat16


def _unit_rms(x):
    return x * lax.rsqrt(jnp.mean(x * x, axis=-1, keepdims=True) + NORM_EPS)


def _dot(a, b):
    return jnp.dot(a, b, preferred_element_type=F32)


def _layer_spec(layer, shape):
    idx = (layer,) + (0,) * len(shape)
    return pl.BlockSpec((None,) + tuple(shape), lambda *_: idx, pipeline_mode=pl.Buffered(1))


def _ffn_kernel(x_ref, wgu_ref, wd_ref, post_ref, o_ref, act_ref):
    x = x_ref[...]
    xn = _unit_rms(x).astype(BF16)
    for j in range(N_FF_CHUNKS):
        g = _dot(xn, wgu_ref[:, j * FF_CHUNK:(j + 1) * FF_CHUNK])
        u = _dot(xn, wgu_ref[:, D_FF + j * FF_CHUNK:D_FF + (j + 1) * FF_CHUNK])
        act_ref[:, j * FF_CHUNK:(j + 1) * FF_CHUNK] = (g * jax.nn.sigmoid(g) * u).astype(BF16)
    out = _dot(act_ref[...], wd_ref[...])
    o_ref[...] = x + _unit_rms(out) * (0.5 * post_ref[...])


def _ffn_call(x2d, layer, w_gu, w_down, post_g, tm):
    n_tiles = x2d.shape[0] // tm
    return pl.pallas_call(
        _ffn_kernel,
        out_shape=jax.ShapeDtypeStruct(x2d.shape, F32),
        grid=(n_tiles,),
        in_specs=[
            pl.BlockSpec((tm, D_MODEL), lambda i: (i, 0)),
            _layer_spec(layer, (D_MODEL, 2 * D_FF)),
            _layer_spec(layer, (D_FF, D_MODEL)),
            _layer_spec(layer, (1, D_MODEL)),
        ],
        out_specs=pl.BlockSpec((tm, D_MODEL), lambda i: (i, 0)),
        scratch_shapes=[pltpu.VMEM((tm, D_FF), BF16)],
        compiler_params=pltpu.CompilerParams(
            dimension_semantics=("arbitrary",), vmem_limit_bytes=VMEM_LIMIT_BYTES),
        name="ffn",
    )(x2d, w_gu, w_down, post_g)


def _lru_scan(a, u, h0, a_s, b_s):
    t = a.shape[0]
    n_groups = t // SUBLANES
    halves = []
    lasts = []
    for c in range(W_A // LANES):
        a_s[c] = a[:, c * LANES:(c + 1) * LANES]
        b_s[c] = u[:, c * LANES:(c + 1) * LANES]
        pa = pu = None
        for r in range(SUBLANES):
            rows_r = pl.ds(r, n_groups, stride=SUBLANES)
            ar = a_s[c, rows_r, :]
            ur = b_s[c, rows_r, :]
            if r == 0:
                pa, pu = ar, ur
            else:
                pu = ar * pu + ur
                pa = ar * pa
                a_s[c, rows_r, :] = pa
                b_s[c, rows_r, :] = pu
        h = h0[:, c * LANES:(c + 1) * LANES]
        for g in range(n_groups):
            i = g * SUBLANES
            hg = a_s[c, i:i + SUBLANES, :] * h + b_s[c, i:i + SUBLANES, :]
            b_s[c, i:i + SUBLANES, :] = hg
            h = hg[SUBLANES - 1:SUBLANES, :]
        halves.append(b_s[c])
        lasts.append(h)
    return jnp.concatenate(halves, axis=-1), jnp.concatenate(lasts, axis=-1)


def _residues(halo, n_taps):
    return tuple(sorted({(halo - (n_taps - 1) + k) % SUBLANES for k in range(n_taps)} - {0}))


def _fill_shifted(ext_ref, sh_ref, residues):
    ext = ext_ref[...]
    n_rows = ext.shape[0]
    for n, s in enumerate(residues):
        sh_ref[n] = pltpu.roll(ext, n_rows - s, axis=0)


def _causal_dwconv(ext_ref, sh_ref, residues, w_ref, bias, halo, n_taps, r0, rows):
    acc = bias
    for k in range(n_taps):
        off = halo - (n_taps - 1) + k
        s = off % SUBLANES
        start = r0 + off - s
        if s == 0:
            tap = ext_ref[start:start + rows, :]
        else:
            tap = sh_ref[residues.index(s), start:start + rows, :]
        acc = acc + jnp.concatenate([w_ref[k]] * (rows // SUBLANES), axis=0) * tap
    return acc


def _gelu_tanh(x):
    return 0.5 * x * (1.0 + jnp.tanh(math.sqrt(2.0 / math.pi) * (x + 0.044715 * (x * x * x))))


def _mixer_kernel(layer, sinks_ref, x_ref, w_in_ref, lcw_ref, lcb_ref, wg_ref, bg_ref, lam_ref,
                  cw_ref, cb_ref, lng_ref, lnb_ref, w_out_ref, post_ref, o_ref,
                  lx_ext, lx_sh, glu_ext, glu_sh, kx, vx, h_ref, a_s, b_s, yb_ref, ycat_ref):
    t_idx = pl.program_id(1)
    tt = x_ref.shape[0]

    @pl.when(t_idx == 0)
    def _():
        lx_ext[0:LX_HALO, :] = jnp.zeros((LX_HALO, W_A), F32)
        glu_ext[0:GLU_HALO, :] = jnp.zeros((GLU_HALO, W_C), F32)
        kx[:, 0:BLK, :] = jnp.zeros((4, BLK, LANES), BF16)
        vx[:, 0:BLK, :] = jnp.zeros((4, BLK, LANES), BF16)
        h_ref[...] = jnp.zeros((1, W_A), F32)

    x = x_ref[...]
    hn = _unit_rms(x).astype(BF16)
    proj = _dot(hn, w_in_ref[...])

    gate = proj[:, W_A:OFF_Q]
    lx_ext[LX_HALO:LX_HALO + tt, :] = proj[:, :W_A]
    lx_res = _residues(LX_HALO, LRU_CONV)
    _fill_shifted(lx_ext, lx_sh, lx_res)
    xc = _causal_dwconv(lx_ext, lx_sh, lx_res, lcw_ref, lcb_ref[...], LX_HALO, LRU_CONV, 0, tt)
    lx_ext[0:LX_HALO, :] = lx_ext[tt:tt + LX_HALO, :]
    gr = _dot(xc.astype(BF16), wg_ref[...]) + bg_ref[...]
    r = jax.nn.sigmoid(gr[:, :W_A])
    i_gate = jax.nn.sigmoid(gr[:, W_A:])
    z = -lam_ref[...]
    softplus = jnp.maximum(z, 0.0) + jnp.log1p(jnp.exp(-jnp.abs(z)))
    log_a = (-LRU_C) * r * softplus
    a = jnp.exp(log_a)
    th = jnp.tanh(log_a)
    u = jnp.sqrt(-2.0 * th / (1.0 - th)) * (i_gate * xc)
    h, h_last = _lru_scan(a, u, h_ref[...], a_s, b_s)
    h_ref[...] = h_last
    ycat_ref[:, 0:W_A] = _unit_rms(_gelu_tanh(gate) * h).astype(BF16)

    glu_ext[GLU_HALO:GLU_HALO + tt, :] = (proj[:, OFF_GLU:OFF_GLU + W_C]
                                          * jax.nn.sigmoid(proj[:, OFF_GLU + W_C:]))
    glu_res = _residues(GLU_HALO, C_CONV)
    _fill_shifted(glu_ext, glu_sh, glu_res)

    def conformer_rows(r0):
        yc = _causal_dwconv(glu_ext, glu_sh, glu_res, cw_ref, cb_ref[...], GLU_HALO, C_CONV, r0, CONV_ROWS)
        mu = jnp.mean(yc, axis=-1, keepdims=True)
        ycc = yc - mu
        var = jnp.mean(ycc * ycc, axis=-1, keepdims=True)
        yc = ycc * lax.rsqrt(var + LN_EPS) * lng_ref[...] + lnb_ref[...]
        y_c = yc * jax.nn.sigmoid(yc)
        ycat_ref[r0:r0 + CONV_ROWS, W_A + W_B:] = _unit_rms(y_c).astype(BF16)

    q = (proj[:, OFF_Q:OFF_K] * (LOG2_E / math.sqrt(HEAD_DIM))).astype(BF16)
    lane = lax.broadcasted_iota(jnp.int32, (tt, LANES), 1)
    lo = lane < HEAD_DIM
    for src, dst in ((proj[:, OFF_K:OFF_V], kx), (proj[:, OFF_V:OFF_GLU], vx)):
        rot = pltpu.roll(src, HEAD_DIM, axis=1)
        dst[0, BLK:BLK + tt, :] = jnp.where(lo, src, 0.0).astype(BF16)
        dst[1, BLK:BLK + tt, :] = jnp.where(lo, 0.0, rot).astype(BF16)
        dst[2, BLK:BLK + tt, :] = jnp.where(lo, rot, 0.0).astype(BF16)
        dst[3, BLK:BLK + tt, :] = jnp.where(lo, 0.0, src).astype(BF16)

    qi = lax.broadcasted_iota(jnp.int32, (2 * BLK, 2 * BLK), 0) & (BLK - 1)
    kj = lax.broadcasted_iota(jnp.int32, (2 * BLK, 2 * BLK), 1)
    rel = BLK + qi - kj
    band = (rel >= 0) & (rel < WINDOW)
    top = lax.broadcasted_iota(jnp.int32, (2 * BLK, 1), 0) < BLK
    for j in range(tt // BLK):
        if j == 0:
            mask = band & ((kj + (t_idx * tt - BLK)) >= 0)
        else:
            mask = band
        for g in range(N_KV_HEADS):
            p0 = 2 * g * LANES
            lhs = jnp.concatenate([q[j * BLK:(j + 1) * BLK, p0:p0 + LANES],
                                   q[j * BLK:(j + 1) * BLK, p0 + LANES:p0 + 2 * LANES]], axis=0)
            o = jnp.zeros((2 * BLK, LANES), F32)
            for e in range(2):
                kk = kx[2 * g + e, j * BLK:(j + 2) * BLK, :]
                vv = vx[2 * g + e, j * BLK:(j + 2) * BLK, :]
                s = lax.dot_general(lhs, kk, (((1,), (1,)), ((), ())), preferred_element_type=F32)
                s = jnp.where(mask, s, NEG_BIG)
                sink = jnp.where(top, sinks_ref[layer, 4 * g + e], sinks_ref[layer, 4 * g + 2 + e]) * LOG2_E
                m = jnp.max(s, axis=-1, keepdims=True)
                p = jnp.exp2(s - m)
                denom = jnp.sum(p, axis=-1, keepdims=True) + jnp.exp2(sink - m)
                o = o + _dot(p.astype(BF16), vv) * (1.0 / denom)
            yb_ref[j * BLK:(j + 1) * BLK, p0:p0 + LANES] = o[:BLK]
            yb_ref[j * BLK:(j + 1) * BLK, p0 + LANES:p0 + 2 * LANES] = o[BLK:]
        for r0 in range(j * BLK, (j + 1) * BLK, CONV_ROWS):
            conformer_rows(r0)
    kx[:, 0:BLK, :] = kx[:, tt:tt + BLK, :]
    vx[:, 0:BLK, :] = vx[:, tt:tt + BLK, :]
    glu_ext[0:GLU_HALO, :] = glu_ext[tt:tt + GLU_HALO, :]
    ycat_ref[:, W_A:W_A + W_B] = _unit_rms(yb_ref[...]).astype(BF16)

    o_ref[...] = x + _unit_rms(_dot(ycat_ref[...], w_out_ref[...])) * post_ref[...]


def _mixer_call(x, layer, sinks, w_in, lcw, lcb, wg, bg, lam, cw, cb, lng, lnb, w_out, post_g, tt):
    bsz, s, _ = x.shape
    vec = lambda n: _layer_spec(layer, (1, n))
    return pl.pallas_call(
        functools.partial(_mixer_kernel, layer),
        out_shape=jax.ShapeDtypeStruct(x.shape, F32),
        grid=(bsz, s // tt),
        in_specs=[
            pl.BlockSpec(memory_space=pltpu.SMEM),
            pl.BlockSpec((None, tt, D_MODEL), lambda b, t: (b, t, 0)),
            _layer_spec(layer, (D_MODEL, D_IN_PROJ)),
            _layer_spec(layer, (LRU_CONV, SUBLANES, W_A)),
            vec(W_A),
            _layer_spec(layer, (W_A, 2 * W_A)),
            vec(2 * W_A),
            vec(W_A),
            _layer_spec(layer, (C_CONV, SUBLANES, W_C)),
            vec(W_C),
            vec(W_C),
            vec(W_C),
            _layer_spec(layer, (D_MIX, D_MODEL)),
            vec(D_MODEL),
        ],
        out_specs=pl.BlockSpec((None, tt, D_MODEL), lambda b, t: (b, t, 0)),
        scratch_shapes=[
            pltpu.VMEM((LX_HALO + tt, W_A), F32),
            pltpu.VMEM((len(_residues(LX_HALO, LRU_CONV)), LX_HALO + tt, W_A), F32),
            pltpu.VMEM((GLU_HALO + tt, W_C), F32),
            pltpu.VMEM((len(_residues(GLU_HALO, C_CONV)), GLU_HALO + tt, W_C), F32),
            pltpu.VMEM((4, BLK + tt, LANES), BF16),
            pltpu.VMEM((4, BLK + tt, LANES), BF16),
            pltpu.VMEM((1, W_A), F32),
            pltpu.VMEM((W_A // LANES, tt, LANES), F32),
            pltpu.VMEM((W_A // LANES, tt, LANES), F32),
            pltpu.VMEM((tt, W_B), F32),
            pltpu.VMEM((tt, D_MIX), BF16),
        ],
        compiler_params=pltpu.CompilerParams(
            dimension_semantics=("arbitrary", "arbitrary"), vmem_limit_bytes=VMEM_LIMIT_BYTES),
        name="mixer",
    )(sinks, x, w_in, lcw, lcb, wg, bg, lam, cw, cb, lng, lnb, w_out, post_g)


def _block_diag(w):
    eye = jnp.eye(A_BLOCKS, dtype=w.dtype)
    return jnp.einsum("lhij,hg->lhigj", w, eye).reshape(w.shape[0], W_A, W_A)


def _pick_tile(n, target):
    t = min(n, target)
    while n % t:
        t //= 2
    return t


def kernel(x, ffn1_pre_g, ffn1_w_gu, ffn1_w_down, ffn1_post_g, mix_pre_g, w_in, lru_conv_w, lru_conv_b, lru_w_a, lru_b_a, lru_w_x, lru_b_x, lru_lambda, attn_sinks, conv_w, conv_b, conv_ln_g, conv_ln_b, group_g, w_out, mix_post_g, ffn2_pre_g, ffn2_w_gu, ffn2_w_down, ffn2_post_g):
    bsz, s, d = x.shape
    depth = w_in.shape[0]
    tm = _pick_tile(bsz * s, 1024)
    tt = _pick_tile(s, 1024)
    rows = lambda v: v.reshape(depth, 1, -1)
    bf = lambda w: w.astype(BF16)
    scaled = lambda g, w: bf(g[:, :, None] * w)
    taps = lambda w: jnp.broadcast_to(w[:, :, None, :], w.shape[:2] + (SUBLANES,) + w.shape[2:])
    ffn1 = (scaled(ffn1_pre_g, ffn1_w_gu), bf(ffn1_w_down), rows(ffn1_post_g))
    ffn2 = (scaled(ffn2_pre_g, ffn2_w_gu), bf(ffn2_w_down), rows(ffn2_post_g))
    w_gates = bf(jnp.concatenate([_block_diag(lru_w_a), _block_diag(lru_w_x)], axis=2))
    b_gates = rows(jnp.concatenate([lru_b_a, lru_b_x], axis=1))
    mix = (attn_sinks, scaled(mix_pre_g, w_in), taps(lru_conv_w), rows(lru_conv_b), w_gates, b_gates,
           rows(lru_lambda), taps(conv_w), rows(conv_b), rows(conv_ln_g), rows(conv_ln_b),
           scaled(group_g, w_out), rows(mix_post_g))
    x2d = x.reshape(bsz * s, d)
    for l in range(depth):
        x2d = _ffn_call(x2d, l, *ffn1, tm)
        x2d = _mixer_call(x2d.reshape(bsz, s, d), l, *mix, tt).reshape(bsz * s, d)
        x2d = _ffn_call(x2d, l, *ffn2, tm)
    return x2d.reshape(bsz, s, d)
```

```python
import functools
import math
from typing import NamedTuple, Optional

import jax
import jax.numpy as jnp
from jax import lax
from jax.experimental import pallas as pl
from jax.experimental.pallas import tpu as pltpu

D_MODEL = 1024
D_FF = 2816
HEAD_DIM = 64
W_A = 256
A_BLOCKS = 4
A_BLOCK_W = W_A // A_BLOCKS
LRU_C = 8.0
LRU_CONV = 4
N_Q_HEADS = 8
N_KV_HEADS = 2
W_B = N_Q_HEADS * HEAD_DIM
WINDOW = 128
BLK = 128
W_C = 256
C_CONV = 31
D_MIX = W_A + W_B + W_C
OFF_Q = 2 * W_A
OFF_K = OFF_Q + W_B
OFF_V = OFF_K + N_KV_HEADS * HEAD_DIM
OFF_GLU = OFF_V + N_KV_HEADS * HEAD_DIM
D_IN_PROJ = OFF_GLU + 2 * W_C
NORM_EPS = 1e-6
LN_EPS = 1e-5
NEG_BIG = -1e30
LOG2_E = math.log2(math.e)

SUBLANES = 8
LANES = 128
MXU_DIM = 256

FF_CHUNK = MXU_DIM
N_FF_CHUNKS = D_FF // FF_CHUNK
LX_HALO = SUBLANES
GLU_HALO = 4 * SUBLANES
CONV_ROWS = 64
VMEM_LIMIT_BYTES = 52 * 1024 * 1024

F32 = jnp.float32
BF16 = jnp.bfloat16


def _unit_rms(x):
    return x * lax.rsqrt(jnp.mean(x * x, axis=-1, keepdims=True) + NORM_EPS)


def _dot(a, b):
    return jnp.dot(a, b, preferred_element_type=F32)


def _layer_spec(layer, shape):
    idx = (layer,) + (0,) * len(shape)
    return pl.BlockSpec((None,) + tuple(shape), lambda *_: idx, pipeline_mode=pl.Buffered(1))


def _whole_spec(shape):
    zeros = (0,) * len(shape)
    return pl.BlockSpec(tuple(shape), lambda *_: zeros, pipeline_mode=pl.Buffered(1))


class _CastJob(NamedTuple):
    w: jax.Array
    gain: Optional[jax.Array]


def _cast_plumbing(jobs, layer, n_steps, step_of):
    in_specs, operands, out_specs, out_shapes = [], [], [], []
    for job in jobs:
        _, rows, cols = job.w.shape
        slab = rows // n_steps
        in_specs.append(pl.BlockSpec((None, slab, cols), lambda *i: (layer, step_of(*i), 0)))
        operands.append(job.w)
        if job.gain is not None:
            in_specs.append(pl.BlockSpec((None, slab, 1), lambda *i: (layer, step_of(*i), 0)))
            operands.append(job.gain)
        out_specs.append(pl.BlockSpec((slab, cols), lambda *i: (step_of(*i), 0)))
        out_shapes.append(jax.ShapeDtypeStruct((rows, cols), BF16))
    return in_specs, operands, out_specs, out_shapes, tuple(job.gain is not None for job in jobs)


def _run_casts(has_gain, in_refs, out_refs):
    in_refs = list(in_refs)
    for gained, out_ref in zip(has_gain, out_refs):
        w = in_refs.pop(0)[...]
        if gained:
            w = w * in_refs.pop(0)[...]
        out_ref[...] = w.astype(BF16)


def _ffn_kernel(has_gain, x_ref, wgu_ref, wd_ref, post_ref, *refs):
    n_in = len(has_gain) + sum(has_gain)
    cast_in, o_ref, cast_out, act_ref = refs[:n_in], refs[n_in], refs[n_in + 1:-1], refs[-1]
    _run_casts(has_gain, cast_in, cast_out)
    x = x_ref[...]
    xn = _unit_rms(x).astype(BF16)
    for j in range(N_FF_CHUNKS):
        g = _dot(xn, wgu_ref[:, j * FF_CHUNK:(j + 1) * FF_CHUNK])
        u = _dot(xn, wgu_ref[:, D_FF + j * FF_CHUNK:D_FF + (j + 1) * FF_CHUNK])
        act_ref[:, j * FF_CHUNK:(j + 1) * FF_CHUNK] = (g * jax.nn.sigmoid(g) * u).astype(BF16)
    out = _dot(act_ref[...], wd_ref[...])
    o_ref[...] = x + _unit_rms(out) * (0.5 * post_ref[...])


def _ffn_call(x2d, layer, w_gu, w_down, post_g, tm, cast_layer, cast_jobs):
    n_tiles = x2d.shape[0] // tm
    c_in, c_args, c_out, c_shapes, has_gain = _cast_plumbing(cast_jobs, cast_layer, n_tiles, lambda i: i)
    out, *cast = pl.pallas_call(
        functools.partial(_ffn_kernel, has_gain),
        out_shape=[jax.ShapeDtypeStruct(x2d.shape, F32)] + c_shapes,
        grid=(n_tiles,),
        in_specs=[
            pl.BlockSpec((tm, D_MODEL), lambda i: (i, 0)),
            _whole_spec((D_MODEL, 2 * D_FF)),
            _whole_spec((D_FF, D_MODEL)),
            _layer_spec(layer, (1, D_MODEL)),
        ] + c_in,
        out_specs=[pl.BlockSpec((tm, D_MODEL), lambda i: (i, 0))] + c_out,
        scratch_shapes=[pltpu.VMEM((tm, D_FF), BF16)],
        compiler_params=pltpu.CompilerParams(
            dimension_semantics=("arbitrary",), vmem_limit_bytes=VMEM_LIMIT_BYTES),
        name="ffn",
    )(x2d, w_gu, w_down, post_g, *c_args)
    return out, cast


def _lru_scan(a, u, h0, a_s, b_s):
    t = a.shape[0]
    n_groups = t // SUBLANES
    halves = []
    lasts = []
    for c in range(W_A // LANES):
        a_s[c] = a[:, c * LANES:(c + 1) * LANES]
        b_s[c] = u[:, c * LANES:(c + 1) * LANES]
        pa = pu = None
        for r in range(SUBLANES):
            rows_r = pl.ds(r, n_groups, stride=SUBLANES)
            ar = a_s[c, rows_r, :]
            ur = b_s[c, rows_r, :]
            if r == 0:
                pa, pu = ar, ur
            else:
                pu = ar * pu + ur
                pa = ar * pa
                a_s[c, rows_r, :] = pa
                b_s[c, rows_r, :] = pu
        h = h0[:, c * LANES:(c + 1) * LANES]
        for g in range(n_groups):
            i = g * SUBLANES
            hg = a_s[c, i:i + SUBLANES, :] * h + b_s[c, i:i + SUBLANES, :]
            b_s[c, i:i + SUBLANES, :] = hg
            h = hg[SUBLANES - 1:SUBLANES, :]
        halves.append(b_s[c])
        lasts.append(h)
    return jnp.concatenate(halves, axis=-1), jnp.concatenate(lasts, axis=-1)


def _residues(halo, n_taps):
    return tuple(sorted({(halo - (n_taps - 1) + k) % SUBLANES for k in range(n_taps)} - {0}))


def _fill_shifted(ext_ref, sh_ref, residues):
    ext = ext_ref[...]
    n_rows = ext.shape[0]
    for n, s in enumerate(residues):
        sh_ref[n] = pltpu.roll(ext, n_rows - s, axis=0)


def _causal_dwconv(ext_ref, sh_ref, residues, w_ref, bias, halo, n_taps, r0, rows):
    acc = bias
    for k in range(n_taps):
        off = halo - (n_taps - 1) + k
        s = off % SUBLANES
        start = r0 + off - s
        if s == 0:
            tap = ext_ref[start:start + rows, :]
        else:
            tap = sh_ref[residues.index(s), start:start + rows, :]
        acc = acc + jnp.concatenate([w_ref[k]] * (rows // SUBLANES), axis=0) * tap
    return acc


def _gelu_tanh(x):
    return 0.5 * x * (1.0 + jnp.tanh(math.sqrt(2.0 / math.pi) * (x + 0.044715 * (x * x * x))))


def _mixer_kernel(layer, has_gain, sinks_ref, x_ref, w_in_ref, lcw_ref, lcb_ref, wg_ref, bg_ref, lam_ref,
                  cw_ref, cb_ref, lng_ref, lnb_ref, w_out_ref, post_ref, *refs):
    n_in = len(has_gain) + sum(has_gain)
    n_out = len(has_gain)
    cast_in, o_ref, cast_out = refs[:n_in], refs[n_in], refs[n_in + 1:n_in + 1 + n_out]
    lx_ext, lx_sh, glu_ext, glu_sh, kx, vx, h_ref, a_s, b_s, yb_ref, ycat_ref = refs[n_in + 1 + n_out:]
    _run_casts(has_gain, cast_in, cast_out)
    t_idx = pl.program_id(1)
    tt = x_ref.shape[0]

    @pl.when(t_idx == 0)
    def _():
        lx_ext[0:LX_HALO, :] = jnp.zeros((LX_HALO, W_A), F32)
        glu_ext[0:GLU_HALO, :] = jnp.zeros((GLU_HALO, W_C), F32)
        kx[:, 0:BLK, :] = jnp.zeros((4, BLK, LANES), BF16)
        vx[:, 0:BLK, :] = jnp.zeros((4, BLK, LANES), BF16)
        h_ref[...] = jnp.zeros((1, W_A), F32)

    x = x_ref[...]
    hn = _unit_rms(x).astype(BF16)
    proj = _dot(hn, w_in_ref[...])

    gate = proj[:, W_A:OFF_Q]
    lx_ext[LX_HALO:LX_HALO + tt, :] = proj[:, :W_A]
    lx_res = _residues(LX_HALO, LRU_CONV)
    _fill_shifted(lx_ext, lx_sh, lx_res)
    xc = _causal_dwconv(lx_ext, lx_sh, lx_res, lcw_ref, lcb_ref[...], LX_HALO, LRU_CONV, 0, tt)
    lx_ext[0:LX_HALO, :] = lx_ext[tt:tt + LX_HALO, :]
    gr = _dot(xc.astype(BF16), wg_ref[...]) + bg_ref[...]
    r = jax.nn.sigmoid(gr[:, :W_A])
    i_gate = jax.nn.sigmoid(gr[:, W_A:])
    z = -lam_ref[...]
    softplus = jnp.maximum(z, 0.0) + jnp.log1p(jnp.exp(-jnp.abs(z)))
    log_a = (-LRU_C) * r * softplus
    a = jnp.exp(log_a)
    th = jnp.tanh(log_a)
    u = jnp.sqrt(-2.0 * th / (1.0 - th)) * (i_gate * xc)
    h, h_last = _lru_scan(a, u, h_ref[...], a_s, b_s)
    h_ref[...] = h_last
    ycat_ref[:, 0:W_A] = _unit_rms(_gelu_tanh(gate) * h).astype(BF16)

    glu_ext[GLU_HALO:GLU_HALO + tt, :] = (proj[:, OFF_GLU:OFF_GLU + W_C]
                                          * jax.nn.sigmoid(proj[:, OFF_GLU + W_C:]))
    glu_res = _residues(GLU_HALO, C_CONV)
    _fill_shifted(glu_ext, glu_sh, glu_res)

    def conformer_rows(r0):
        yc = _causal_dwconv(glu_ext, glu_sh, glu_res, cw_ref, cb_ref[...], GLU_HALO, C_CONV, r0, CONV_ROWS)
        mu = jnp.mean(yc, axis=-1, keepdims=True)
        ycc = yc - mu
        var = jnp.mean(ycc * ycc, axis=-1, keepdims=True)
        yc = ycc * lax.rsqrt(var + LN_EPS) * lng_ref[...] + lnb_ref[...]
        y_c = yc * jax.nn.sigmoid(yc)
        ycat_ref[r0:r0 + CONV_ROWS, W_A + W_B:] = _unit_rms(y_c).astype(BF16)

    q = (proj[:, OFF_Q:OFF_K] * (LOG2_E / math.sqrt(HEAD_DIM))).astype(BF16)
    lane = lax.broadcasted_iota(jnp.int32, (tt, LANES), 1)
    lo = lane < HEAD_DIM
    for src, dst in ((proj[:, OFF_K:OFF_V], kx), (proj[:, OFF_V:OFF_GLU], vx)):
        rot = pltpu.roll(src, HEAD_DIM, axis=1)
        dst[0, BLK:BLK + tt, :] = jnp.where(lo, src, 0.0).astype(BF16)
        dst[1, BLK:BLK + tt, :] = jnp.where(lo, 0.0, rot).astype(BF16)
        dst[2, BLK:BLK + tt, :] = jnp.where(lo, rot, 0.0).astype(BF16)
        dst[3, BLK:BLK + tt, :] = jnp.where(lo, 0.0, src).astype(BF16)

    qi = lax.broadcasted_iota(jnp.int32, (2 * BLK, 2 * BLK), 0) & (BLK - 1)
    kj = lax.broadcasted_iota(jnp.int32, (2 * BLK, 2 * BLK), 1)
    rel = BLK + qi - kj
    band = (rel >= 0) & (rel < WINDOW)
    top = lax.broadcasted_iota(jnp.int32, (2 * BLK, 1), 0) < BLK
    for j in range(tt // BLK):
        if j == 0:
            mask = band & ((kj + (t_idx * tt - BLK)) >= 0)
        else:
            mask = band
        for g in range(N_KV_HEADS):
            p0 = 2 * g * LANES
            lhs = jnp.concatenate([q[j * BLK:(j + 1) * BLK, p0:p0 + LANES],
                                   q[j * BLK:(j + 1) * BLK, p0 + LANES:p0 + 2 * LANES]], axis=0)
            o = jnp.zeros((2 * BLK, LANES), F32)
            for e in range(2):
                kk = kx[2 * g + e, j * BLK:(j + 2) * BLK, :]
                vv = vx[2 * g + e, j * BLK:(j + 2) * BLK, :]
                s = lax.dot_general(lhs, kk, (((1,), (1,)), ((), ())), preferred_element_type=F32)
                s = jnp.where(mask, s, NEG_BIG)
                sink = jnp.where(top, sinks_ref[layer, 4 * g + e], sinks_ref[layer, 4 * g + 2 + e]) * LOG2_E
                m = jnp.max(s, axis=-1, keepdims=True)
                p = jnp.exp2(s - m)
                denom = jnp.sum(p, axis=-1, keepdims=True) + jnp.exp2(sink - m)
                o = o + _dot(p.astype(BF16), vv) * (1.0 / denom)
            yb_ref[j * BLK:(j + 1) * BLK, p0:p0 + LANES] = o[:BLK]
            yb_ref[j * BLK:(j + 1) * BLK, p0 + LANES:p0 + 2 * LANES] = o[BLK:]
        for r0 in range(j * BLK, (j + 1) * BLK, CONV_ROWS):
            conformer_rows(r0)
    kx[:, 0:BLK, :] = kx[:, tt:tt + BLK, :]
    vx[:, 0:BLK, :] = vx[:, tt:tt + BLK, :]
    glu_ext[0:GLU_HALO, :] = glu_ext[tt:tt + GLU_HALO, :]
    ycat_ref[:, W_A:W_A + W_B] = _unit_rms(yb_ref[...]).astype(BF16)

    o_ref[...] = x + _unit_rms(_dot(ycat_ref[...], w_out_ref[...])) * post_ref[...]


def _mixer_call(x, layer, sinks, w_in, lcw, lcb, wg, bg, lam, cw, cb, lng, lnb, w_out, post_g, tt, cast_jobs):
    bsz, s, _ = x.shape
    n_t = s // tt
    vec = lambda n: _layer_spec(layer, (1, n))
    c_in, c_args, c_out, c_shapes, has_gain = _cast_plumbing(cast_jobs, layer, bsz * n_t, lambda b, t: b * n_t + t)
    out, *cast = pl.pallas_call(
        functools.partial(_mixer_kernel, layer, has_gain),
        out_shape=[jax.ShapeDtypeStruct(x.shape, F32)] + c_shapes,
        grid=(bsz, n_t),
        in_specs=[
            pl.BlockSpec(memory_space=pltpu.SMEM),
            pl.BlockSpec((None, tt, D_MODEL), lambda b, t: (b, t, 0)),
            _whole_spec((D_MODEL, D_IN_PROJ)),
            _layer_spec(layer, (LRU_CONV, SUBLANES, W_A)),
            vec(W_A),
            _layer_spec(layer, (W_A, 2 * W_A)),
            vec(2 * W_A),
            vec(W_A),
            _layer_spec(layer, (C_CONV, SUBLANES, W_C)),
            vec(W_C),
            vec(W_C),
            vec(W_C),
            _whole_spec((D_MIX, D_MODEL)),
            vec(D_MODEL),
        ] + c_in,
        out_specs=[pl.BlockSpec((None, tt, D_MODEL), lambda b, t: (b, t, 0))] + c_out,
        scratch_shapes=[
            pltpu.VMEM((LX_HALO + tt, W_A), F32),
            pltpu.VMEM((len(_residues(LX_HALO, LRU_CONV)), LX_HALO + tt, W_A), F32),
            pltpu.VMEM((GLU_HALO + tt, W_C), F32),
            pltpu.VMEM((len(_residues(GLU_HALO, C_CONV)), GLU_HALO + tt, W_C), F32),
            pltpu.VMEM((4, BLK + tt, LANES), BF16),
            pltpu.VMEM((4, BLK + tt, LANES), BF16),
            pltpu.VMEM((1, W_A), F32),
            pltpu.VMEM((W_A // LANES, tt, LANES), F32),
            pltpu.VMEM((W_A // LANES, tt, LANES), F32),
            pltpu.VMEM((tt, W_B), F32),
            pltpu.VMEM((tt, D_MIX), BF16),
        ],
        compiler_params=pltpu.CompilerParams(
            dimension_semantics=("arbitrary", "arbitrary"), vmem_limit_bytes=VMEM_LIMIT_BYTES),
        name="mixer",
    )(sinks, x, w_in, lcw, lcb, wg, bg, lam, cw, cb, lng, lnb, w_out, post_g, *c_args)
    return out, cast


def _block_diag(w):
    eye = jnp.eye(A_BLOCKS, dtype=w.dtype)
    return jnp.einsum("lhij,hg->lhigj", w, eye).reshape(w.shape[0], W_A, W_A)


def _pick_tile(n, target):
    t = min(n, target)
    while n % t:
        t //= 2
    return t


def kernel(x, ffn1_pre_g, ffn1_w_gu, ffn1_w_down, ffn1_post_g, mix_pre_g, w_in, lru_conv_w, lru_conv_b, lru_w_a, lru_b_a, lru_w_x, lru_b_x, lru_lambda, attn_sinks, conv_w, conv_b, conv_ln_g, conv_ln_b, group_g, w_out, mix_post_g, ffn2_pre_g, ffn2_w_gu, ffn2_w_down, ffn2_post_g):
    bsz, s, d = x.shape
    depth = w_in.shape[0]
    tm = _pick_tile(bsz * s, 1024)
    tt = _pick_tile(s, 1024)
    rows = lambda v: v.reshape(depth, 1, -1)
    col = lambda g: g.reshape(depth, -1, 1)
    taps = lambda w: jnp.broadcast_to(w[:, :, None, :], w.shape[:2] + (SUBLANES,) + w.shape[2:])
    w_gates = jnp.concatenate([_block_diag(lru_w_a), _block_diag(lru_w_x)], axis=2).astype(BF16)
    b_gates = rows(jnp.concatenate([lru_b_a, lru_b_x], axis=1))
    mix_small = (taps(lru_conv_w), rows(lru_conv_b), w_gates, b_gates, rows(lru_lambda), taps(conv_w),
                 rows(conv_b), rows(conv_ln_g), rows(conv_ln_b))
    mixer_jobs = [_CastJob(w_in, col(mix_pre_g)), _CastJob(w_out, col(group_g))]
    ffn1_jobs = [_CastJob(ffn1_w_gu, col(ffn1_pre_g)), _CastJob(ffn1_w_down.reshape(depth, D_MODEL, D_FF), None)]
    ffn2_jobs = [_CastJob(ffn2_w_gu, col(ffn2_pre_g)), _CastJob(ffn2_w_down.reshape(depth, D_MODEL, D_FF), None)]
    w_gu = (ffn1_pre_g[0][:, None] * ffn1_w_gu[0]).astype(BF16)
    w_down = ffn1_w_down[0].astype(BF16)
    x2d = x.reshape(bsz * s, d)
    for l in range(depth):
        x2d, (w_in_l, w_out_l) = _ffn_call(x2d, l, w_gu, w_down, rows(ffn1_post_g), tm, l, mixer_jobs)
        x3d, (w_gu, w_down) = _mixer_call(x2d.reshape(bsz, s, d), l, attn_sinks, w_in_l, *mix_small, w_out_l,
                                          rows(mix_post_g), tt, ffn2_jobs)
        nxt = ffn1_jobs if l + 1 < depth else []
        x2d, cast = _ffn_call(x3d.reshape(bsz * s, d), l, w_gu, w_down.reshape(D_FF, D_MODEL),
                              rows(ffn2_post_g), tm, l + 1, nxt)
        if nxt:
            w_gu, w_down = cast[0], cast[1].reshape(D_FF, D_MODEL)
    return x2d.reshape(bsz, s, d)
```

```python
import functools
import math
from typing import NamedTuple, Optional

import jax
import jax.numpy as jnp
from jax import lax
from jax.experimental import pallas as pl
from jax.experimental.pallas import tpu as pltpu

D_MODEL = 1024
D_FF = 2816
HEAD_DIM = 64
W_A = 256
A_BLOCKS = 4
A_BLOCK_W = W_A // A_BLOCKS
LRU_C = 8.0
LRU_CONV = 4
N_Q_HEADS = 8
N_KV_HEADS = 2
W_B = N_Q_HEADS * HEAD_DIM
WINDOW = 128
BLK = 128
W_C = 256
C_CONV = 31
D_MIX = W_A + W_B + W_C
OFF_Q = 2 * W_A
OFF_K = OFF_Q + W_B
OFF_V = OFF_K + N_KV_HEADS * HEAD_DIM
OFF_GLU = OFF_V + N_KV_HEADS * HEAD_DIM
D_IN_PROJ = OFF_GLU + 2 * W_C
NORM_EPS = 1e-6
LN_EPS = 1e-5
NEG_BIG = -1e30
LOG2_E = math.log2(math.e)

SUBLANES = 8
BF16_SUBLANES = 16
LANES = 128
MXU_DIM = 256

FF_CHUNK = MXU_DIM
N_FF_CHUNKS = D_FF // FF_CHUNK
LX_HALO = SUBLANES
GLU_HALO = 4 * SUBLANES
CONV_ROWS = 64
VMEM_LIMIT_BYTES = 52 * 1024 * 1024

F32 = jnp.float32
BF16 = jnp.bfloat16


def _unit_rms(x):
    return x * lax.rsqrt(jnp.mean(x * x, axis=-1, keepdims=True) + NORM_EPS)


def _dot(a, b):
    return jnp.dot(a, b, preferred_element_type=F32)


def _layer_spec(layer, shape):
    idx = (layer,) + (0,) * len(shape)
    return pl.BlockSpec((None,) + tuple(shape), lambda *_: idx, pipeline_mode=pl.Buffered(1))


def _whole_spec(shape):
    zeros = (0,) * len(shape)
    return pl.BlockSpec(tuple(shape), lambda *_: zeros, pipeline_mode=pl.Buffered(1))


class _CastJob(NamedTuple):
    w: jax.Array
    gain: Optional[jax.Array]


def _cast_plumbing(jobs, layer, n_steps, step_of):
    in_specs, operands, out_specs, out_shapes = [], [], [], []
    for job in jobs:
        _, rows, cols = job.w.shape
        slab = min(r for r in range(BF16_SUBLANES, rows + 1, BF16_SUBLANES)
                   if rows % r == 0 and rows // r <= n_steps)
        block = functools.partial(lambda last, *i: jnp.minimum(step_of(*i), last), rows // slab - 1)
        in_specs.append(pl.BlockSpec((None, slab, cols), lambda *i, block=block: (layer, block(*i), 0)))
        operands.append(job.w)
        if job.gain is not None:
            in_specs.append(pl.BlockSpec((None, slab, 1), lambda *i, block=block: (layer, block(*i), 0)))
            operands.append(job.gain)
        out_specs.append(pl.BlockSpec((slab, cols), lambda *i, block=block: (block(*i), 0)))
        out_shapes.append(jax.ShapeDtypeStruct((rows, cols), BF16))
    return in_specs, operands, out_specs, out_shapes, tuple(job.gain is not None for job in jobs)


def _run_casts(has_gain, in_refs, out_refs):
    in_refs = list(in_refs)
    for gained, out_ref in zip(has_gain, out_refs):
        w = in_refs.pop(0)[...]
        if gained:
            w = w * in_refs.pop(0)[...]
        out_ref[...] = w.astype(BF16)


def _ffn_kernel(has_gain, x_ref, wgu_ref, wd_ref, post_ref, *refs):
    n_in = len(has_gain) + sum(has_gain)
    cast_in, o_ref, cast_out, act_ref = refs[:n_in], refs[n_in], refs[n_in + 1:-1], refs[-1]
    _run_casts(has_gain, cast_in, cast_out)
    x = x_ref[...]
    xn = _unit_rms(x).astype(BF16)
    for j in range(N_FF_CHUNKS):
        g = _dot(xn, wgu_ref[:, j * FF_CHUNK:(j + 1) * FF_CHUNK])
        u = _dot(xn, wgu_ref[:, D_FF + j * FF_CHUNK:D_FF + (j + 1) * FF_CHUNK])
        act_ref[:, j * FF_CHUNK:(j + 1) * FF_CHUNK] = (g * jax.nn.sigmoid(g) * u).astype(BF16)
    out = _dot(act_ref[...], wd_ref[...])
    o_ref[...] = x + _unit_rms(out) * (0.5 * post_ref[...])


def _ffn_call(x2d, layer, w_gu, w_down, post_g, tm, cast_layer, cast_jobs):
    n_tiles = x2d.shape[0] // tm
    c_in, c_args, c_out, c_shapes, has_gain = _cast_plumbing(cast_jobs, cast_layer, n_tiles, lambda i: i)
    out, *cast = pl.pallas_call(
        functools.partial(_ffn_kernel, has_gain),
        out_shape=[jax.ShapeDtypeStruct(x2d.shape, F32)] + c_shapes,
        grid=(n_tiles,),
        in_specs=[
            pl.BlockSpec((tm, D_MODEL), lambda i: (i, 0)),
            _whole_spec((D_MODEL, 2 * D_FF)),
            _whole_spec((D_FF, D_MODEL)),
            _layer_spec(layer, (1, D_MODEL)),
        ] + c_in,
        out_specs=[pl.BlockSpec((tm, D_MODEL), lambda i: (i, 0))] + c_out,
        scratch_shapes=[pltpu.VMEM((tm, D_FF), BF16)],
        compiler_params=pltpu.CompilerParams(
            dimension_semantics=("arbitrary",), vmem_limit_bytes=VMEM_LIMIT_BYTES),
        name="ffn",
    )(x2d, w_gu, w_down, post_g, *c_args)
    return out, cast


def _lru_scan(a, u, h0, a_s, b_s):
    t = a.shape[0]
    n_groups = t // SUBLANES
    halves = []
    lasts = []
    for c in range(W_A // LANES):
        a_s[c] = a[:, c * LANES:(c + 1) * LANES]
        b_s[c] = u[:, c * LANES:(c + 1) * LANES]
        pa = pu = None
        for r in range(SUBLANES):
            rows_r = pl.ds(r, n_groups, stride=SUBLANES)
            ar = a_s[c, rows_r, :]
            ur = b_s[c, rows_r, :]
            if r == 0:
                pa, pu = ar, ur
            else:
                pu = ar * pu + ur
                pa = ar * pa
                a_s[c, rows_r, :] = pa
                b_s[c, rows_r, :] = pu
        h = h0[:, c * LANES:(c + 1) * LANES]
        for g in range(n_groups):
            i = g * SUBLANES
            hg = a_s[c, i:i + SUBLANES, :] * h + b_s[c, i:i + SUBLANES, :]
            b_s[c, i:i + SUBLANES, :] = hg
            h = hg[SUBLANES - 1:SUBLANES, :]
        halves.append(b_s[c])
        lasts.append(h)
    return jnp.concatenate(halves, axis=-1), jnp.concatenate(lasts, axis=-1)


def _residues(halo, n_taps):
    return tuple(sorted({(halo - (n_taps - 1) + k) % SUBLANES for k in range(n_taps)} - {0}))


def _fill_shifted(ext_ref, sh_ref, residues):
    ext = ext_ref[...]
    n_rows = ext.shape[0]
    for n, s in enumerate(residues):
        sh_ref[n] = pltpu.roll(ext, n_rows - s, axis=0)


def _causal_dwconv(ext_ref, sh_ref, residues, w_ref, bias, halo, n_taps, r0, rows):
    acc = bias
    for k in range(n_taps):
        off = halo - (n_taps - 1) + k
        s = off % SUBLANES
        start = r0 + off - s
        if s == 0:
            tap = ext_ref[start:start + rows, :]
        else:
            tap = sh_ref[residues.index(s), start:start + rows, :]
        acc = acc + jnp.concatenate([w_ref[k]] * (rows // SUBLANES), axis=0) * tap
    return acc


def _gelu_tanh(x):
    return 0.5 * x * (1.0 + jnp.tanh(math.sqrt(2.0 / math.pi) * (x + 0.044715 * (x * x * x))))


def _mixer_kernel(layer, has_gain, sinks_ref, x_ref, w_in_ref, lcw_ref, lcb_ref, wg_ref, bg_ref, lam_ref,
                  cw_ref, cb_ref, lng_ref, lnb_ref, w_out_ref, post_ref, *refs):
    n_in = len(has_gain) + sum(has_gain)
    n_out = len(has_gain)
    cast_in, o_ref, cast_out = refs[:n_in], refs[n_in], refs[n_in + 1:n_in + 1 + n_out]
    lx_ext, lx_sh, glu_ext, glu_sh, kx, vx, h_ref, a_s, b_s, yb_ref, ycat_ref = refs[n_in + 1 + n_out:]
    _run_casts(has_gain, cast_in, cast_out)
    t_idx = pl.program_id(1)
    tt = x_ref.shape[0]

    @pl.when(t_idx == 0)
    def _():
        lx_ext[0:LX_HALO, :] = jnp.zeros((LX_HALO, W_A), F32)
        glu_ext[0:GLU_HALO, :] = jnp.zeros((GLU_HALO, W_C), F32)
        kx[:, 0:BLK, :] = jnp.zeros((4, BLK, LANES), BF16)
        vx[:, 0:BLK, :] = jnp.zeros((4, BLK, LANES), BF16)
        h_ref[...] = jnp.zeros((1, W_A), F32)

    x = x_ref[...]
    hn = _unit_rms(x).astype(BF16)
    proj = _dot(hn, w_in_ref[...])

    gate = proj[:, W_A:OFF_Q]
    lx_ext[LX_HALO:LX_HALO + tt, :] = proj[:, :W_A]
    lx_res = _residues(LX_HALO, LRU_CONV)
    _fill_shifted(lx_ext, lx_sh, lx_res)
    xc = _causal_dwconv(lx_ext, lx_sh, lx_res, lcw_ref, lcb_ref[...], LX_HALO, LRU_CONV, 0, tt)
    lx_ext[0:LX_HALO, :] = lx_ext[tt:tt + LX_HALO, :]
    gr = _dot(xc.astype(BF16), wg_ref[...]) + bg_ref[...]
    r = jax.nn.sigmoid(gr[:, :W_A])
    i_gate = jax.nn.sigmoid(gr[:, W_A:])
    z = -lam_ref[...]
    softplus = jnp.maximum(z, 0.0) + jnp.log1p(jnp.exp(-jnp.abs(z)))
    log_a = (-LRU_C) * r * softplus
    a = jnp.exp(log_a)
    th = jnp.tanh(log_a)
    u = jnp.sqrt(-2.0 * th / (1.0 - th)) * (i_gate * xc)
    h, h_last = _lru_scan(a, u, h_ref[...], a_s, b_s)
    h_ref[...] = h_last
    ycat_ref[:, 0:W_A] = _unit_rms(_gelu_tanh(gate) * h).astype(BF16)

    glu_ext[GLU_HALO:GLU_HALO + tt, :] = (proj[:, OFF_GLU:OFF_GLU + W_C]
                                          * jax.nn.sigmoid(proj[:, OFF_GLU + W_C:]))
    glu_res = _residues(GLU_HALO, C_CONV)
    _fill_shifted(glu_ext, glu_sh, glu_res)

    def conformer_rows(r0):
        yc = _causal_dwconv(glu_ext, glu_sh, glu_res, cw_ref, cb_ref[...], GLU_HALO, C_CONV, r0, CONV_ROWS)
        mu = jnp.mean(yc, axis=-1, keepdims=True)
        ycc = yc - mu
        var = jnp.mean(ycc * ycc, axis=-1, keepdims=True)
        yc = ycc * lax.rsqrt(var + LN_EPS) * lng_ref[...] + lnb_ref[...]
        y_c = yc * jax.nn.sigmoid(yc)
        ycat_ref[r0:r0 + CONV_ROWS, W_A + W_B:] = _unit_rms(y_c).astype(BF16)

    q = (proj[:, OFF_Q:OFF_K] * (LOG2_E / math.sqrt(HEAD_DIM))).astype(BF16)
    lane = lax.broadcasted_iota(jnp.int32, (tt, LANES), 1)
    lo = lane < HEAD_DIM
    for src, dst in ((proj[:, OFF_K:OFF_V], kx), (proj[:, OFF_V:OFF_GLU], vx)):
        rot = pltpu.roll(src, HEAD_DIM, axis=1)
        dst[0, BLK:BLK + tt, :] = jnp.where(lo, src, 0.0).astype(BF16)
        dst[1, BLK:BLK + tt, :] = jnp.where(lo, 0.0, rot).astype(BF16)
        dst[2, BLK:BLK + tt, :] = jnp.where(lo, rot, 0.0).astype(BF16)
        dst[3, BLK:BLK + tt, :] = jnp.where(lo, 0.0, src).astype(BF16)

    qi = lax.broadcasted_iota(jnp.int32, (2 * BLK, 2 * BLK), 0) & (BLK - 1)
    kj = lax.broadcasted_iota(jnp.int32, (2 * BLK, 2 * BLK), 1)
    rel = BLK + qi - kj
    band = (rel >= 0) & (rel < WINDOW)
    top = lax.broadcasted_iota(jnp.int32, (2 * BLK, 1), 0) < BLK
    for j in range(tt // BLK):
        if j == 0:
            mask = band & ((kj + (t_idx * tt - BLK)) >= 0)
        else:
            mask = band
        for g in range(N_KV_HEADS):
            p0 = 2 * g * LANES
            lhs = jnp.concatenate([q[j * BLK:(j + 1) * BLK, p0:p0 + LANES],
                                   q[j * BLK:(j + 1) * BLK, p0 + LANES:p0 + 2 * LANES]], axis=0)
            o = jnp.zeros((2 * BLK, LANES), F32)
            for e in range(2):
                kk = kx[2 * g + e, j * BLK:(j + 2) * BLK, :]
                vv = vx[2 * g + e, j * BLK:(j + 2) * BLK, :]
                s = lax.dot_general(lhs, kk, (((1,), (1,)), ((), ())), preferred_element_type=F32)
                s = jnp.where(mask, s, NEG_BIG)
                sink = jnp.where(top, sinks_ref[layer, 4 * g + e], sinks_ref[layer, 4 * g + 2 + e]) * LOG2_E
                m = jnp.max(s, axis=-1, keepdims=True)
                p = jnp.exp2(s - m)
                denom = jnp.sum(p, axis=-1, keepdims=True) + jnp.exp2(sink - m)
                o = o + _dot(p.astype(BF16), vv) * (1.0 / denom)
            yb_ref[j * BLK:(j + 1) * BLK, p0:p0 + LANES] = o[:BLK]
            yb_ref[j * BLK:(j + 1) * BLK, p0 + LANES:p0 + 2 * LANES] = o[BLK:]
        for r0 in range(j * BLK, (j + 1) * BLK, CONV_ROWS):
            conformer_rows(r0)
    kx[:, 0:BLK, :] = kx[:, tt:tt + BLK, :]
    vx[:, 0:BLK, :] = vx[:, tt:tt + BLK, :]
    glu_ext[0:GLU_HALO, :] = glu_ext[tt:tt + GLU_HALO, :]
    ycat_ref[:, W_A:W_A + W_B] = _unit_rms(yb_ref[...]).astype(BF16)

    o_ref[...] = x + _unit_rms(_dot(ycat_ref[...], w_out_ref[...])) * post_ref[...]


def _mixer_call(x, layer, sinks, w_in, lcw, lcb, wg, bg, lam, cw, cb, lng, lnb, w_out, post_g, tt, cast_jobs):
    bsz, s, _ = x.shape
    n_t = s // tt
    vec = lambda n: _layer_spec(layer, (1, n))
    c_in, c_args, c_out, c_shapes, has_gain = _cast_plumbing(cast_jobs, layer, bsz * n_t, lambda b, t: b * n_t + t)
    out, *cast = pl.pallas_call(
        functools.partial(_mixer_kernel, layer, has_gain),
        out_shape=[jax.ShapeDtypeStruct(x.shape, F32)] + c_shapes,
        grid=(bsz, n_t),
        in_specs=[
            pl.BlockSpec(memory_space=pltpu.SMEM),
            pl.BlockSpec((None, tt, D_MODEL), lambda b, t: (b, t, 0)),
            _whole_spec((D_MODEL, D_IN_PROJ)),
            _layer_spec(layer, (LRU_CONV, SUBLANES, W_A)),
            vec(W_A),
            _layer_spec(layer, (W_A, 2 * W_A)),
            vec(2 * W_A),
            vec(W_A),
            _layer_spec(layer, (C_CONV, SUBLANES, W_C)),
            vec(W_C),
            vec(W_C),
            vec(W_C),
            _whole_spec((D_MIX, D_MODEL)),
            vec(D_MODEL),
        ] + c_in,
        out_specs=[pl.BlockSpec((None, tt, D_MODEL), lambda b, t: (b, t, 0))] + c_out,
        scratch_shapes=[
            pltpu.VMEM((LX_HALO + tt, W_A), F32),
            pltpu.VMEM((len(_residues(LX_HALO, LRU_CONV)), LX_HALO + tt, W_A), F32),
            pltpu.VMEM((GLU_HALO + tt, W_C), F32),
            pltpu.VMEM((len(_residues(GLU_HALO, C_CONV)), GLU_HALO + tt, W_C), F32),
            pltpu.VMEM((4, BLK + tt, LANES), BF16),
            pltpu.VMEM((4, BLK + tt, LANES), BF16),
            pltpu.VMEM((1, W_A), F32),
            pltpu.VMEM((W_A // LANES, tt, LANES), F32),
            pltpu.VMEM((W_A // LANES, tt, LANES), F32),
            pltpu.VMEM((tt, W_B), F32),
            pltpu.VMEM((tt, D_MIX), BF16),
        ],
        compiler_params=pltpu.CompilerParams(
            dimension_semantics=("arbitrary", "arbitrary"), vmem_limit_bytes=VMEM_LIMIT_BYTES),
        name="mixer",
    )(sinks, x, w_in, lcw, lcb, wg, bg, lam, cw, cb, lng, lnb, w_out, post_g, *c_args)
    return out, cast


def _block_diag(w):
    eye = jnp.eye(A_BLOCKS, dtype=w.dtype)
    return jnp.einsum("lhij,hg->lhigj", w, eye).reshape(w.shape[0], W_A, W_A)


def _pick_tile(n, target):
    t = min(n, target)
    while n % t:
        t //= 2
    return t


def kernel(x, ffn1_pre_g, ffn1_w_gu, ffn1_w_down, ffn1_post_g, mix_pre_g, w_in, lru_conv_w, lru_conv_b, lru_w_a, lru_b_a, lru_w_x, lru_b_x, lru_lambda, attn_sinks, conv_w, conv_b, conv_ln_g, conv_ln_b, group_g, w_out, mix_post_g, ffn2_pre_g, ffn2_w_gu, ffn2_w_down, ffn2_post_g):
    bsz, s, d = x.shape
    depth = w_in.shape[0]
    tm = _pick_tile(bsz * s, 1024)
    tt = _pick_tile(s, 1024)
    rows = lambda v: v.reshape(depth, 1, -1)
    col = lambda g: g.reshape(depth, -1, 1)
    taps = lambda w: jnp.broadcast_to(w[:, :, None, :], w.shape[:2] + (SUBLANES,) + w.shape[2:])
    w_gates = jnp.concatenate([_block_diag(lru_w_a), _block_diag(lru_w_x)], axis=2).astype(BF16)
    b_gates = rows(jnp.concatenate([lru_b_a, lru_b_x], axis=1))
    mix_small = (taps(lru_conv_w), rows(lru_conv_b), w_gates, b_gates, rows(lru_lambda), taps(conv_w),
                 rows(conv_b), rows(conv_ln_g), rows(conv_ln_b))
    mixer_jobs = [_CastJob(w_in, col(mix_pre_g)), _CastJob(w_out, col(group_g))]
    ffn1_jobs = [_CastJob(ffn1_w_gu, col(ffn1_pre_g)), _CastJob(ffn1_w_down, None)]
    ffn2_jobs = [_CastJob(ffn2_w_gu, col(ffn2_pre_g)), _CastJob(ffn2_w_down, None)]
    w_gu = (ffn1_pre_g[0][:, None] * ffn1_w_gu[0]).astype(BF16)
    w_down = ffn1_w_down[0].astype(BF16)
    x2d = x.reshape(bsz * s, d)
    for l in range(depth):
        x2d, (w_in_l, w_out_l) = _ffn_call(x2d, l, w_gu, w_down, rows(ffn1_post_g), tm, l, mixer_jobs)
        x3d, (w_gu, w_down) = _mixer_call(x2d.reshape(bsz, s, d), l, attn_sinks, w_in_l, *mix_small, w_out_l,
                                          rows(mix_post_g), tt, ffn2_jobs)
        nxt = ffn1_jobs if l + 1 < depth else []
        x2d, cast = _ffn_call(x3d.reshape(bsz * s, d), l, w_gu, w_down, rows(ffn2_post_g), tm, l + 1, nxt)
        if nxt:
            w_gu, w_down = cast
    return x2d.reshape(bsz, s, d)
```

```python
import functools
import math
from typing import NamedTuple, Optional

import jax
import jax.numpy as jnp
from jax import lax
from jax.experimental import pallas as pl
from jax.experimental.pallas import tpu as pltpu

D_MODEL = 1024
D_FF = 2816
HEAD_DIM = 64
W_A = 256
A_BLOCKS = 4
A_BLOCK_W = W_A // A_BLOCKS
LRU_C = 8.0
LRU_CONV = 4
N_Q_HEADS = 8
N_KV_HEADS = 2
W_B = N_Q_HEADS * HEAD_DIM
WINDOW = 128
BLK = 128
W_C = 256
C_CONV = 31
D_MIX = W_A + W_B + W_C
OFF_Q = 2 * W_A
OFF_K = OFF_Q + W_B
OFF_V = OFF_K + N_KV_HEADS * HEAD_DIM
OFF_GLU = OFF_V + N_KV_HEADS * HEAD_DIM
D_IN_PROJ = OFF_GLU + 2 * W_C
NORM_EPS = 1e-6
LN_EPS = 1e-5
NEG_BIG = -1e30
LOG2_E = math.log2(math.e)

SUBLANES = 8
BF16_SUBLANES = 16
LANES = 128
MXU_DIM = 256

FF_CHUNK = MXU_DIM
N_FF_CHUNKS = D_FF // FF_CHUNK
LX_HALO = SUBLANES
GLU_HALO = 4 * SUBLANES
CONV_ROWS = 64
VMEM_LIMIT_BYTES = 52 * 1024 * 1024

F32 = jnp.float32
BF16 = jnp.bfloat16


def _unit_rms(x):
    return x * lax.rsqrt(jnp.mean(x * x, axis=-1, keepdims=True) + NORM_EPS)


def _dot(a, b):
    return jnp.dot(a, b, preferred_element_type=F32)


def _layer_spec(layer, shape):
    idx = (layer,) + (0,) * len(shape)
    return pl.BlockSpec((None,) + tuple(shape), lambda *_: idx, pipeline_mode=pl.Buffered(1))


def _whole_spec(shape):
    zeros = (0,) * len(shape)
    return pl.BlockSpec(tuple(shape), lambda *_: zeros, pipeline_mode=pl.Buffered(1))


class _CastJob(NamedTuple):
    w: jax.Array
    gain: Optional[jax.Array]


def _cast_plumbing(jobs, layer, n_steps, step_of):
    in_specs, operands, out_specs, out_shapes = [], [], [], []
    for job in jobs:
        _, rows, cols = job.w.shape
        slab = min(r for r in range(BF16_SUBLANES, rows + 1, BF16_SUBLANES)
                   if rows % r == 0 and rows // r <= n_steps)
        block = functools.partial(lambda last, *i: jnp.minimum(step_of(*i), last), rows // slab - 1)
        in_specs.append(pl.BlockSpec((None, slab, cols), lambda *i, block=block: (layer, block(*i), 0)))
        operands.append(job.w)
        if job.gain is not None:
            in_specs.append(pl.BlockSpec((None, slab, 1), lambda *i, block=block: (layer, block(*i), 0)))
            operands.append(job.gain)
        out_specs.append(pl.BlockSpec((slab, cols), lambda *i, block=block: (block(*i), 0)))
        out_shapes.append(jax.ShapeDtypeStruct((rows, cols), BF16))
    return in_specs, operands, out_specs, out_shapes, tuple(job.gain is not None for job in jobs)


def _run_casts(has_gain, in_refs, out_refs):
    in_refs = list(in_refs)
    for gained, out_ref in zip(has_gain, out_refs):
        w = in_refs.pop(0)[...]
        if gained:
            w = w * in_refs.pop(0)[...]
        out_ref[...] = w.astype(BF16)


def _ffn_kernel(has_gain, x_ref, wgu_ref, wd_ref, post_ref, *refs):
    n_in = len(has_gain) + sum(has_gain)
    cast_in, o_ref, cast_out, act_ref = refs[:n_in], refs[n_in], refs[n_in + 1:-1], refs[-1]
    _run_casts(has_gain, cast_in, cast_out)
    x = x_ref[...]
    xn = _unit_rms(x).astype(BF16)
    for j in range(N_FF_CHUNKS):
        g = _dot(xn, wgu_ref[:, j * FF_CHUNK:(j + 1) * FF_CHUNK])
        u = _dot(xn, wgu_ref[:, D_FF + j * FF_CHUNK:D_FF + (j + 1) * FF_CHUNK])
        act_ref[:, j * FF_CHUNK:(j + 1) * FF_CHUNK] = (g * jax.nn.sigmoid(g) * u).astype(BF16)
    out = _dot(act_ref[...], wd_ref[...])
    o_ref[...] = x + _unit_rms(out) * (0.5 * post_ref[...])


def _ffn_call(x2d, layer, w_gu, w_down, post_g, tm, cast_layer, cast_jobs):
    n_tiles = x2d.shape[0] // tm
    c_in, c_args, c_out, c_shapes, has_gain = _cast_plumbing(cast_jobs, cast_layer, n_tiles, lambda i: i)
    out, *cast = pl.pallas_call(
        functools.partial(_ffn_kernel, has_gain),
        out_shape=[jax.ShapeDtypeStruct(x2d.shape, F32)] + c_shapes,
        grid=(n_tiles,),
        in_specs=[
            pl.BlockSpec((tm, D_MODEL), lambda i: (i, 0)),
            _whole_spec((D_MODEL, 2 * D_FF)),
            _whole_spec((D_FF, D_MODEL)),
            _layer_spec(layer, (1, D_MODEL)),
        ] + c_in,
        out_specs=[pl.BlockSpec((tm, D_MODEL), lambda i: (i, 0))] + c_out,
        scratch_shapes=[pltpu.VMEM((tm, D_FF), BF16)],
        compiler_params=pltpu.CompilerParams(
            dimension_semantics=("arbitrary",), vmem_limit_bytes=VMEM_LIMIT_BYTES),
        name="ffn",
    )(x2d, w_gu, w_down, post_g, *c_args)
    return out, cast


def _lru_scan(a, u, h0, a_s, b_s):
    t = a.shape[0]
    n_groups = t // SUBLANES
    halves = []
    lasts = []
    for c in range(W_A // LANES):
        a_s[c] = a[:, c * LANES:(c + 1) * LANES]
        b_s[c] = u[:, c * LANES:(c + 1) * LANES]
        pa = pu = None
        for r in range(SUBLANES):
            rows_r = pl.ds(r, n_groups, stride=SUBLANES)
            ar = a_s[c, rows_r, :]
            ur = b_s[c, rows_r, :]
            if r == 0:
                pa, pu = ar, ur
            else:
                pu = ar * pu + ur
                pa = ar * pa
                a_s[c, rows_r, :] = pa
                b_s[c, rows_r, :] = pu
        h = h0[:, c * LANES:(c + 1) * LANES]
        for g in range(n_groups):
            i = g * SUBLANES
            hg = a_s[c, i:i + SUBLANES, :] * h + b_s[c, i:i + SUBLANES, :]
            b_s[c, i:i + SUBLANES, :] = hg
            h = hg[SUBLANES - 1:SUBLANES, :]
        halves.append(b_s[c])
        lasts.append(h)
    return jnp.concatenate(halves, axis=-1), jnp.concatenate(lasts, axis=-1)


def _residues(halo, n_taps):
    return tuple(sorted({(halo - (n_taps - 1) + k) % SUBLANES for k in range(n_taps)} - {0}))


def _fill_shifted(ext_ref, sh_ref, residues):
    ext = ext_ref[...]
    n_rows = ext.shape[0]
    for n, s in enumerate(residues):
        sh_ref[n] = pltpu.roll(ext, n_rows - s, axis=0)


def _causal_dwconv(ext_ref, sh_ref, residues, w_ref, bias, halo, n_taps, r0, rows):
    acc = bias
    for k in range(n_taps):
        off = halo - (n_taps - 1) + k
        s = off % SUBLANES
        start = r0 + off - s
        if s == 0:
            tap = ext_ref[start:start + rows, :]
        else:
            tap = sh_ref[residues.index(s), start:start + rows, :]
        acc = acc + jnp.concatenate([w_ref[k]] * (rows // SUBLANES), axis=0) * tap
    return acc


def _gelu_tanh(x):
    return 0.5 * x * (1.0 + jnp.tanh(math.sqrt(2.0 / math.pi) * (x + 0.044715 * (x * x * x))))


def _mixer_kernel(layer, has_gain, sinks_ref, x_ref, w_in_ref, lcw_ref, lcb_ref, wg_ref, bg_ref, lam_ref,
                  cw_ref, cb_ref, lng_ref, lnb_ref, w_out_ref, post_ref, *refs):
    n_in = len(has_gain) + sum(has_gain)
    n_out = len(has_gain)
    cast_in, o_ref, cast_out = refs[:n_in], refs[n_in], refs[n_in + 1:n_in + 1 + n_out]
    lx_ext, lx_sh, glu_ext, glu_sh, kx, vx, h_ref, a_s, b_s, yb_ref, ycat_ref = refs[n_in + 1 + n_out:]
    _run_casts(has_gain, cast_in, cast_out)
    t_idx = pl.program_id(1)
    tt = x_ref.shape[0]

    @pl.when(t_idx == 0)
    def _():
        lx_ext[0:LX_HALO, :] = jnp.zeros((LX_HALO, W_A), F32)
        glu_ext[0:GLU_HALO, :] = jnp.zeros((GLU_HALO, W_C), F32)
        kx[:, 0:BLK, :] = jnp.zeros((4, BLK, LANES), BF16)
        vx[:, 0:BLK, :] = jnp.zeros((4, BLK, LANES), BF16)
        h_ref[...] = jnp.zeros((1, W_A), F32)

    x = x_ref[...]
    hn = _unit_rms(x).astype(BF16)
    proj = _dot(hn, w_in_ref[...])

    gate = proj[:, W_A:OFF_Q]
    lx_ext[LX_HALO:LX_HALO + tt, :] = proj[:, :W_A]
    lx_res = _residues(LX_HALO, LRU_CONV)
    _fill_shifted(lx_ext, lx_sh, lx_res)
    xc = _causal_dwconv(lx_ext, lx_sh, lx_res, lcw_ref, lcb_ref[...], LX_HALO, LRU_CONV, 0, tt)
    lx_ext[0:LX_HALO, :] = lx_ext[tt:tt + LX_HALO, :]
    gr = _dot(xc.astype(BF16), wg_ref[...]) + bg_ref[...]
    r = jax.nn.sigmoid(gr[:, :W_A])
    i_gate = jax.nn.sigmoid(gr[:, W_A:])
    z = -lam_ref[...]
    softplus = jnp.maximum(z, 0.0) + jnp.log1p(jnp.exp(-jnp.abs(z)))
    log_a = (-LRU_C) * r * softplus
    a = jnp.exp(log_a)
    th = jnp.tanh(log_a)
    u = jnp.sqrt(-2.0 * th / (1.0 - th)) * (i_gate * xc)
    h, h_last = _lru_scan(a, u, h_ref[...], a_s, b_s)
    h_ref[...] = h_last
    ycat_ref[:, 0:W_A] = _unit_rms(_gelu_tanh(gate) * h).astype(BF16)

    glu_ext[GLU_HALO:GLU_HALO + tt, :] = (proj[:, OFF_GLU:OFF_GLU + W_C]
                                          * jax.nn.sigmoid(proj[:, OFF_GLU + W_C:]))
    glu_res = _residues(GLU_HALO, C_CONV)
    _fill_shifted(glu_ext, glu_sh, glu_res)

    def conformer_rows(r0):
        yc = _causal_dwconv(glu_ext, glu_sh, glu_res, cw_ref, cb_ref[...], GLU_HALO, C_CONV, r0, CONV_ROWS)
        mu = jnp.mean(yc, axis=-1, keepdims=True)
        ycc = yc - mu
        var = jnp.mean(ycc * ycc, axis=-1, keepdims=True)
        yc = ycc * lax.rsqrt(var + LN_EPS) * lng_ref[...] + lnb_ref[...]
        y_c = yc * jax.nn.sigmoid(yc)
        ycat_ref[r0:r0 + CONV_ROWS, W_A + W_B:] = _unit_rms(y_c).astype(BF16)

    q = (proj[:, OFF_Q:OFF_K] * (LOG2_E / math.sqrt(HEAD_DIM))).astype(BF16)
    lane = lax.broadcasted_iota(jnp.int32, (tt, LANES), 1)
    lo = lane < HEAD_DIM
    for src, dst in ((proj[:, OFF_K:OFF_V], kx), (proj[:, OFF_V:OFF_GLU], vx)):
        rot = pltpu.roll(src, HEAD_DIM, axis=1)
        dst[0, BLK:BLK + tt, :] = jnp.where(lo, src, 0.0).astype(BF16)
        dst[1, BLK:BLK + tt, :] = jnp.where(lo, 0.0, rot).astype(BF16)
        dst[2, BLK:BLK + tt, :] = jnp.where(lo, rot, 0.0).astype(BF16)
        dst[3, BLK:BLK + tt, :] = jnp.where(lo, 0.0, src).astype(BF16)

    qi = lax.broadcasted_iota(jnp.int32, (2 * BLK, 2 * BLK), 0) & (BLK - 1)
    kj = lax.broadcasted_iota(jnp.int32, (2 * BLK, 2 * BLK), 1)
    rel = BLK + qi - kj
    band = (rel >= 0) & (rel < WINDOW)
    top = lax.broadcasted_iota(jnp.int32, (2 * BLK, 1), 0) < BLK
    for j in range(tt // BLK):
        if j == 0:
            mask = band & ((kj + (t_idx * tt - BLK)) >= 0)
        else:
            mask = band
        for g in range(N_KV_HEADS):
            p0 = 2 * g * LANES
            lhs = jnp.concatenate([q[j * BLK:(j + 1) * BLK, p0:p0 + LANES],
                                   q[j * BLK:(j + 1) * BLK, p0 + LANES:p0 + 2 * LANES]], axis=0)
            o = jnp.zeros((2 * BLK, LANES), F32)
            for e in range(2):
                kk = kx[2 * g + e, j * BLK:(j + 2) * BLK, :]
                vv = vx[2 * g + e, j * BLK:(j + 2) * BLK, :]
                s = lax.dot_general(lhs, kk, (((1,), (1,)), ((), ())), preferred_element_type=F32)
                s = jnp.where(mask, s, NEG_BIG)
                sink = jnp.where(top, sinks_ref[layer, 4 * g + e], sinks_ref[layer, 4 * g + 2 + e]) * LOG2_E
                m = jnp.max(s, axis=-1, keepdims=True)
                p = jnp.exp2(s - m)
                denom = jnp.sum(p, axis=-1, keepdims=True) + jnp.exp2(sink - m)
                o = o + _dot(p.astype(BF16), vv) * (1.0 / denom)
            yb_ref[j * BLK:(j + 1) * BLK, p0:p0 + LANES] = o[:BLK]
            yb_ref[j * BLK:(j + 1) * BLK, p0 + LANES:p0 + 2 * LANES] = o[BLK:]
        for r0 in range(j * BLK, (j + 1) * BLK, CONV_ROWS):
            conformer_rows(r0)
    kx[:, 0:BLK, :] = kx[:, tt:tt + BLK, :]
    vx[:, 0:BLK, :] = vx[:, tt:tt + BLK, :]
    glu_ext[0:GLU_HALO, :] = glu_ext[tt:tt + GLU_HALO, :]
    ycat_ref[:, W_A:W_A + W_B] = _unit_rms(yb_ref[...]).astype(BF16)

    o_ref[...] = x + _unit_rms(_dot(ycat_ref[...], w_out_ref[...])) * post_ref[...]


def _mixer_call(x, layer, sinks, w_in, lcw, lcb, wg, bg, lam, cw, cb, lng, lnb, w_out, post_g, tt, cast_jobs):
    bsz, s, _ = x.shape
    n_t = s // tt
    vec = lambda n: _layer_spec(layer, (1, n))
    c_in, c_args, c_out, c_shapes, has_gain = _cast_plumbing(cast_jobs, layer, bsz * n_t, lambda b, t: b * n_t + t)
    out, *cast = pl.pallas_call(
        functools.partial(_mixer_kernel, layer, has_gain),
        out_shape=[jax.ShapeDtypeStruct(x.shape, F32)] + c_shapes,
        grid=(bsz, n_t),
        in_specs=[
            pl.BlockSpec(memory_space=pltpu.SMEM),
            pl.BlockSpec((None, tt, D_MODEL), lambda b, t: (b, t, 0)),
            _whole_spec((D_MODEL, D_IN_PROJ)),
            _layer_spec(layer, (LRU_CONV, SUBLANES, W_A)),
            vec(W_A),
            _layer_spec(layer, (W_A, 2 * W_A)),
            vec(2 * W_A),
            vec(W_A),
            _layer_spec(layer, (C_CONV, SUBLANES, W_C)),
            vec(W_C),
            vec(W_C),
            vec(W_C),
            _whole_spec((D_MIX, D_MODEL)),
            vec(D_MODEL),
        ] + c_in,
        out_specs=[pl.BlockSpec((None, tt, D_MODEL), lambda b, t: (b, t, 0))] + c_out,
        scratch_shapes=[
            pltpu.VMEM((LX_HALO + tt, W_A), F32),
            pltpu.VMEM((len(_residues(LX_HALO, LRU_CONV)), LX_HALO + tt, W_A), F32),
            pltpu.VMEM((GLU_HALO + tt, W_C), F32),
            pltpu.VMEM((len(_residues(GLU_HALO, C_CONV)), GLU_HALO + tt, W_C), F32),
            pltpu.VMEM((4, BLK + tt, LANES), BF16),
            pltpu.VMEM((4, BLK + tt, LANES), BF16),
            pltpu.VMEM((1, W_A), F32),
            pltpu.VMEM((W_A // LANES, tt, LANES), F32),
            pltpu.VMEM((W_A // LANES, tt, LANES), F32),
            pltpu.VMEM((tt, W_B), F32),
            pltpu.VMEM((tt, D_MIX), BF16),
        ],
        compiler_params=pltpu.CompilerParams(
            dimension_semantics=("arbitrary", "arbitrary"), vmem_limit_bytes=VMEM_LIMIT_BYTES),
        name="mixer",
    )(sinks, x, w_in, lcw, lcb, wg, bg, lam, cw, cb, lng, lnb, w_out, post_g, *c_args)
    return out, cast


def _block_diag(w):
    eye = jnp.eye(A_BLOCKS, dtype=w.dtype)
    return jnp.einsum("lhij,hg->lhigj", w, eye).reshape(w.shape[0], W_A, W_A)


def _pick_tile(n, target):
    t = min(n, target)
    while n % t:
        t //= 2
    return t


def kernel(x, ffn1_pre_g, ffn1_w_gu, ffn1_w_down, ffn1_post_g, mix_pre_g, w_in, lru_conv_w, lru_conv_b, lru_w_a, lru_b_a, lru_w_x, lru_b_x, lru_lambda, attn_sinks, conv_w, conv_b, conv_ln_g, conv_ln_b, group_g, w_out, mix_post_g, ffn2_pre_g, ffn2_w_gu, ffn2_w_down, ffn2_post_g):
    bsz, s, d = x.shape
    depth = w_in.shape[0]
    tm = _pick_tile(bsz * s, 1024)
    tt = _pick_tile(s, 1024)
    rows = lambda v: v.reshape(depth, 1, -1)
    col = lambda g: g.reshape(depth, -1, 1)
    taps = lambda w: jnp.broadcast_to(w[:, :, None, :], w.shape[:2] + (SUBLANES,) + w.shape[2:])
    w_gates = jnp.concatenate([_block_diag(lru_w_a), _block_diag(lru_w_x)], axis=2).astype(BF16)
    b_gates = rows(jnp.concatenate([lru_b_a, lru_b_x], axis=1))
    mix_small = (taps(lru_conv_w), rows(lru_conv_b), w_gates, b_gates, rows(lru_lambda), taps(conv_w),
                 rows(conv_b), rows(conv_ln_g), rows(conv_ln_b))
    mixer_jobs = [_CastJob(w_in, col(mix_pre_g)), _CastJob(w_out, col(group_g))]
    ffn1_jobs = [_CastJob(ffn1_w_gu, col(ffn1_pre_g)), _CastJob(ffn1_w_down, None)]
    ffn2_jobs = [_CastJob(ffn2_w_gu, col(ffn2_pre_g)), _CastJob(ffn2_w_down, None)]
    w_gu = (ffn1_pre_g[0][:, None] * ffn1_w_gu[0]).astype(BF16)
    w_down = ffn1_w_down[0].astype(BF16)
    x2d = x.reshape(bsz * s, d)
    for l in range(depth):
        x2d, (w_in_l, w_out_l, w_gu, w_down) = _ffn_call(x2d, l, w_gu, w_down, rows(ffn1_post_g), tm, l,
                                                         mixer_jobs + ffn2_jobs)
        x3d, _ = _mixer_call(x2d.reshape(bsz, s, d), l, attn_sinks, w_in_l, *mix_small, w_out_l,
                             rows(mix_post_g), tt, [])
        nxt = ffn1_jobs if l + 1 < depth else []
        x2d, cast = _ffn_call(x3d.reshape(bsz * s, d), l, w_gu, w_down, rows(ffn2_post_g), tm, l + 1, nxt)
        if nxt:
            w_gu, w_down = cast
    return x2d.reshape(bsz, s, d)
```

```python
import functools
import math
from typing import NamedTuple, Optional

import jax
import jax.numpy as jnp
from jax import lax
from jax.experimental import pallas as pl
from jax.experimental.pallas import tpu as pltpu

D_MODEL = 1024
D_FF = 2816
HEAD_DIM = 64
W_A = 256
A_BLOCKS = 4
A_BLOCK_W = W_A // A_BLOCKS
LRU_C = 8.0
LRU_CONV = 4
N_Q_HEADS = 8
N_KV_HEADS = 2
W_B = N_Q_HEADS * HEAD_DIM
WINDOW = 128
BLK = 128
W_C = 256
C_CONV = 31
D_MIX = W_A + W_B + W_C
OFF_Q = 2 * W_A
OFF_K = OFF_Q + W_B
OFF_V = OFF_K + N_KV_HEADS * HEAD_DIM
OFF_GLU = OFF_V + N_KV_HEADS * HEAD_DIM
D_IN_PROJ = OFF_GLU + 2 * W_C
NORM_EPS = 1e-6
LN_EPS = 1e-5
NEG_BIG = -1e30
LOG2_E = math.log2(math.e)

SUBLANES = 8
BF16_SUBLANES = 16
LANES = 128
MXU_DIM = 256

FF_CHUNK = MXU_DIM
N_FF_CHUNKS = D_FF // FF_CHUNK
DOWN_PIECES = 4
PROJ_PIECES = 4
LX_HALO = SUBLANES
GLU_HALO = 4 * SUBLANES
CONV_ROWS = 64
VMEM_LIMIT_BYTES = 52 * 1024 * 1024

F32 = jnp.float32
BF16 = jnp.bfloat16


def _unit_rms(x):
    return x * lax.rsqrt(jnp.mean(x * x, axis=-1, keepdims=True) + NORM_EPS)


def _dot(a, b):
    return jnp.dot(a, b, preferred_element_type=F32)


def _layer_spec(layer, shape):
    idx = (layer,) + (0,) * len(shape)
    return pl.BlockSpec((None,) + tuple(shape), lambda *_: idx, pipeline_mode=pl.Buffered(1))


def _whole_spec(shape):
    zeros = (0,) * len(shape)
    return pl.BlockSpec(tuple(shape), lambda *_: zeros, pipeline_mode=pl.Buffered(1))


class _CastJob(NamedTuple):
    w: jax.Array
    gain: Optional[jax.Array]


def _cast_plumbing(jobs, layer, n_steps, step_of):
    in_specs, operands, out_specs, out_shapes = [], [], [], []
    for job in jobs:
        _, rows, cols = job.w.shape
        slab = min(r for r in range(BF16_SUBLANES, rows + 1, BF16_SUBLANES)
                   if rows % r == 0 and rows // r <= n_steps)
        block = functools.partial(lambda last, *i: jnp.minimum(step_of(*i), last), rows // slab - 1)
        in_specs.append(pl.BlockSpec((None, slab, cols), lambda *i, block=block: (layer, block(*i), 0)))
        operands.append(job.w)
        if job.gain is not None:
            in_specs.append(pl.BlockSpec((None, slab, 1), lambda *i, block=block: (layer, block(*i), 0)))
            operands.append(job.gain)
        out_specs.append(pl.BlockSpec((slab, cols), lambda *i, block=block: (block(*i), 0)))
        out_shapes.append(jax.ShapeDtypeStruct((rows, cols), BF16))
    return in_specs, operands, out_specs, out_shapes, tuple(job.gain is not None for job in jobs)


def _run_casts(has_gain, in_refs, out_refs):
    in_refs = list(in_refs)
    for gained, out_ref in zip(has_gain, out_refs):
        w = in_refs.pop(0)[...]
        if gained:
            w = w * in_refs.pop(0)[...]
        out_ref[...] = w.astype(BF16)


def _ffn_kernel(has_gain, x_ref, wgu_ref, wd_ref, post_ref, *refs):
    n_in = len(has_gain) + sum(has_gain)
    cast_in, o_ref, cast_out, act_ref = refs[:n_in], refs[n_in], refs[n_in + 1:-1], refs[-1]
    _run_casts(has_gain, cast_in, cast_out)
    x = x_ref[...]
    xb = x.astype(BF16)
    r = lax.rsqrt(jnp.mean(x * x, axis=-1, keepdims=True) + NORM_EPS)
    for j in range(N_FF_CHUNKS):
        g = _dot(xb, wgu_ref[:, j * FF_CHUNK:(j + 1) * FF_CHUNK]) * r
        u = _dot(xb, wgu_ref[:, D_FF + j * FF_CHUNK:D_FF + (j + 1) * FF_CHUNK]) * r
        act_ref[:, j * FF_CHUNK:(j + 1) * FF_CHUNK] = (g * jax.nn.sigmoid(g) * u).astype(BF16)
    rows = x.shape[0] // DOWN_PIECES
    half_gain = 0.5 * post_ref[...]
    for i in range(DOWN_PIECES):
        out = _dot(act_ref[i * rows:(i + 1) * rows, :], wd_ref[...])
        o_ref[i * rows:(i + 1) * rows, :] = x_ref[i * rows:(i + 1) * rows, :] + _unit_rms(out) * half_gain


def _ffn_call(x2d, layer, w_gu, w_down, post_g, tm, cast_layer, cast_jobs):
    n_tiles = x2d.shape[0] // tm
    c_in, c_args, c_out, c_shapes, has_gain = _cast_plumbing(cast_jobs, cast_layer, n_tiles, lambda i: i)
    out, *cast = pl.pallas_call(
        functools.partial(_ffn_kernel, has_gain),
        out_shape=[jax.ShapeDtypeStruct(x2d.shape, F32)] + c_shapes,
        grid=(n_tiles,),
        in_specs=[
            pl.BlockSpec((tm, D_MODEL), lambda i: (i, 0)),
            _whole_spec((D_MODEL, 2 * D_FF)),
            _whole_spec((D_FF, D_MODEL)),
            _layer_spec(layer, (1, D_MODEL)),
        ] + c_in,
        out_specs=[pl.BlockSpec((tm, D_MODEL), lambda i: (i, 0))] + c_out,
        scratch_shapes=[pltpu.VMEM((tm, D_FF), BF16)],
        compiler_params=pltpu.CompilerParams(
            dimension_semantics=("arbitrary",), vmem_limit_bytes=VMEM_LIMIT_BYTES),
        name="ffn",
    )(x2d, w_gu, w_down, post_g, *c_args)
    return out, cast


def _lru_scan(a, u, h0, a_s, b_s):
    t = a.shape[0]
    n_groups = t // SUBLANES
    halves = []
    lasts = []
    for c in range(W_A // LANES):
        a_s[c] = a[:, c * LANES:(c + 1) * LANES]
        b_s[c] = u[:, c * LANES:(c + 1) * LANES]
        pa = pu = None
        for r in range(SUBLANES):
            rows_r = pl.ds(r, n_groups, stride=SUBLANES)
            ar = a_s[c, rows_r, :]
            ur = b_s[c, rows_r, :]
            if r == 0:
                pa, pu = ar, ur
            else:
                pu = ar * pu + ur
                pa = ar * pa
                a_s[c, rows_r, :] = pa
                b_s[c, rows_r, :] = pu
        h = h0[:, c * LANES:(c + 1) * LANES]
        for g in range(n_groups):
            i = g * SUBLANES
            hg = a_s[c, i:i + SUBLANES, :] * h + b_s[c, i:i + SUBLANES, :]
            b_s[c, i:i + SUBLANES, :] = hg
            h = hg[SUBLANES - 1:SUBLANES, :]
        halves.append(b_s[c])
        lasts.append(h)
    return jnp.concatenate(halves, axis=-1), jnp.concatenate(lasts, axis=-1)


def _residues(halo, n_taps):
    return tuple(sorted({(halo - (n_taps - 1) + k) % SUBLANES for k in range(n_taps)} - {0}))


def _fill_shifted(ext_ref, sh_ref, residues):
    ext = ext_ref[...]
    n_rows = ext.shape[0]
    for n, s in enumerate(residues):
        sh_ref[n] = pltpu.roll(ext, n_rows - s, axis=0)


def _causal_dwconv(ext_ref, sh_ref, residues, w_ref, bias, halo, n_taps, r0, rows):
    acc = bias
    for k in range(n_taps):
        off = halo - (n_taps - 1) + k
        s = off % SUBLANES
        start = r0 + off - s
        if s == 0:
            tap = ext_ref[start:start + rows, :]
        else:
            tap = sh_ref[residues.index(s), start:start + rows, :]
        acc = acc + jnp.concatenate([w_ref[k]] * (rows // SUBLANES), axis=0) * tap
    return acc


def _gelu_tanh(x):
    return 0.5 * x * (1.0 + jnp.tanh(math.sqrt(2.0 / math.pi) * (x + 0.044715 * (x * x * x))))


def _mixer_kernel(layer, has_gain, sinks_ref, x_ref, w_in_ref, lcw_ref, lcb_ref, wg_ref, bg_ref, lam_ref,
                  cw_ref, cb_ref, lng_ref, lnb_ref, w_out_ref, post_ref, *refs):
    n_in = len(has_gain) + sum(has_gain)
    n_out = len(has_gain)
    cast_in, o_ref, cast_out = refs[:n_in], refs[n_in], refs[n_in + 1:n_in + 1 + n_out]
    lx_ext, lx_sh, glu_ext, glu_sh, kx, vx, h_ref, a_s, b_s, yb_ref, ycat_ref = refs[n_in + 1 + n_out:]
    _run_casts(has_gain, cast_in, cast_out)
    t_idx = pl.program_id(1)
    tt = x_ref.shape[0]

    @pl.when(t_idx == 0)
    def _():
        lx_ext[0:LX_HALO, :] = jnp.zeros((LX_HALO, W_A), F32)
        glu_ext[0:GLU_HALO, :] = jnp.zeros((GLU_HALO, W_C), F32)
        kx[:, 0:BLK, :] = jnp.zeros((4, BLK, LANES), BF16)
        vx[:, 0:BLK, :] = jnp.zeros((4, BLK, LANES), BF16)
        h_ref[...] = jnp.zeros((1, W_A), F32)

    rows = tt // PROJ_PIECES
    proj = jnp.concatenate([_dot(_unit_rms(x_ref[i * rows:(i + 1) * rows, :]).astype(BF16), w_in_ref[...])
                            for i in range(PROJ_PIECES)], axis=0)

    gate = proj[:, W_A:OFF_Q]
    lx_ext[LX_HALO:LX_HALO + tt, :] = proj[:, :W_A]
    lx_res = _residues(LX_HALO, LRU_CONV)
    _fill_shifted(lx_ext, lx_sh, lx_res)
    xc = _causal_dwconv(lx_ext, lx_sh, lx_res, lcw_ref, lcb_ref[...], LX_HALO, LRU_CONV, 0, tt)
    lx_ext[0:LX_HALO, :] = lx_ext[tt:tt + LX_HALO, :]
    gr = _dot(xc.astype(BF16), wg_ref[...]) + bg_ref[...]
    r = jax.nn.sigmoid(gr[:, :W_A])
    i_gate = jax.nn.sigmoid(gr[:, W_A:])
    z = -lam_ref[...]
    softplus = jnp.maximum(z, 0.0) + jnp.log1p(jnp.exp(-jnp.abs(z)))
    log_a = (-LRU_C) * r * softplus
    a = jnp.exp(log_a)
    th = jnp.tanh(log_a)
    u = jnp.sqrt(-2.0 * th / (1.0 - th)) * (i_gate * xc)
    h, h_last = _lru_scan(a, u, h_ref[...], a_s, b_s)
    h_ref[...] = h_last
    ycat_ref[:, 0:W_A] = _unit_rms(_gelu_tanh(gate) * h).astype(BF16)

    glu_ext[GLU_HALO:GLU_HALO + tt, :] = (proj[:, OFF_GLU:OFF_GLU + W_C]
                                          * jax.nn.sigmoid(proj[:, OFF_GLU + W_C:]))
    glu_res = _residues(GLU_HALO, C_CONV)
    _fill_shifted(glu_ext, glu_sh, glu_res)

    def conformer_rows(r0):
        yc = _causal_dwconv(glu_ext, glu_sh, glu_res, cw_ref, cb_ref[...], GLU_HALO, C_CONV, r0, CONV_ROWS)
        mu = jnp.mean(yc, axis=-1, keepdims=True)
        ycc = yc - mu
        var = jnp.mean(ycc * ycc, axis=-1, keepdims=True)
        yc = ycc * lax.rsqrt(var + LN_EPS) * lng_ref[...] + lnb_ref[...]
        y_c = yc * jax.nn.sigmoid(yc)
        ycat_ref[r0:r0 + CONV_ROWS, W_A + W_B:] = _unit_rms(y_c).astype(BF16)

    q = (proj[:, OFF_Q:OFF_K] * (LOG2_E / math.sqrt(HEAD_DIM))).astype(BF16)
    lane = lax.broadcasted_iota(jnp.int32, (tt, LANES), 1)
    lo = lane < HEAD_DIM
    for src, dst in ((proj[:, OFF_K:OFF_V], kx), (proj[:, OFF_V:OFF_GLU], vx)):
        rot = pltpu.roll(src, HEAD_DIM, axis=1)
        dst[0, BLK:BLK + tt, :] = jnp.where(lo, src, 0.0).astype(BF16)
        dst[1, BLK:BLK + tt, :] = jnp.where(lo, 0.0, rot).astype(BF16)
        dst[2, BLK:BLK + tt, :] = jnp.where(lo, rot, 0.0).astype(BF16)
        dst[3, BLK:BLK + tt, :] = jnp.where(lo, 0.0, src).astype(BF16)

    qi = lax.broadcasted_iota(jnp.int32, (2 * BLK, 2 * BLK), 0) & (BLK - 1)
    kj = lax.broadcasted_iota(jnp.int32, (2 * BLK, 2 * BLK), 1)
    rel = BLK + qi - kj
    band = (rel >= 0) & (rel < WINDOW)
    top = lax.broadcasted_iota(jnp.int32, (2 * BLK, 1), 0) < BLK
    for j in range(tt // BLK):
        if j == 0:
            mask = band & ((kj + (t_idx * tt - BLK)) >= 0)
        else:
            mask = band
        for g in range(N_KV_HEADS):
            p0 = 2 * g * LANES
            lhs = jnp.concatenate([q[j * BLK:(j + 1) * BLK, p0:p0 + LANES],
                                   q[j * BLK:(j + 1) * BLK, p0 + LANES:p0 + 2 * LANES]], axis=0)
            o = jnp.zeros((2 * BLK, LANES), F32)
            for e in range(2):
                kk = kx[2 * g + e, j * BLK:(j + 2) * BLK, :]
                vv = vx[2 * g + e, j * BLK:(j + 2) * BLK, :]
                s = lax.dot_general(lhs, kk, (((1,), (1,)), ((), ())), preferred_element_type=F32)
                s = jnp.where(mask, s, NEG_BIG)
                sink = jnp.where(top, sinks_ref[layer, 4 * g + e], sinks_ref[layer, 4 * g + 2 + e]) * LOG2_E
                m = jnp.max(s, axis=-1, keepdims=True)
                p = jnp.exp2(s - m)
                denom = jnp.sum(p, axis=-1, keepdims=True) + jnp.exp2(sink - m)
                o = o + _dot(p.astype(BF16), vv) * (1.0 / denom)
            yb_ref[j * BLK:(j + 1) * BLK, p0:p0 + LANES] = o[:BLK]
            yb_ref[j * BLK:(j + 1) * BLK, p0 + LANES:p0 + 2 * LANES] = o[BLK:]
        for r0 in range(j * BLK, (j + 1) * BLK, CONV_ROWS):
            conformer_rows(r0)
    kx[:, 0:BLK, :] = kx[:, tt:tt + BLK, :]
    vx[:, 0:BLK, :] = vx[:, tt:tt + BLK, :]
    glu_ext[0:GLU_HALO, :] = glu_ext[tt:tt + GLU_HALO, :]
    ycat_ref[:, W_A:W_A + W_B] = _unit_rms(yb_ref[...]).astype(BF16)

    for i in range(PROJ_PIECES):
        piece = slice(i * rows, (i + 1) * rows)
        o_ref[piece, :] = x_ref[piece, :] + _unit_rms(_dot(ycat_ref[piece, :], w_out_ref[...])) * post_ref[...]


def _mixer_call(x, layer, sinks, w_in, lcw, lcb, wg, bg, lam, cw, cb, lng, lnb, w_out, post_g, tt, cast_jobs):
    bsz, s, _ = x.shape
    n_t = s // tt
    vec = lambda n: _layer_spec(layer, (1, n))
    c_in, c_args, c_out, c_shapes, has_gain = _cast_plumbing(cast_jobs, layer, bsz * n_t, lambda b, t: b * n_t + t)
    out, *cast = pl.pallas_call(
        functools.partial(_mixer_kernel, layer, has_gain),
        out_shape=[jax.ShapeDtypeStruct(x.shape, F32)] + c_shapes,
        grid=(bsz, n_t),
        in_specs=[
            pl.BlockSpec(memory_space=pltpu.SMEM),
            pl.BlockSpec((None, tt, D_MODEL), lambda b, t: (b, t, 0)),
            _whole_spec((D_MODEL, D_IN_PROJ)),
            _layer_spec(layer, (LRU_CONV, SUBLANES, W_A)),
            vec(W_A),
            _layer_spec(layer, (W_A, 2 * W_A)),
            vec(2 * W_A),
            vec(W_A),
            _layer_spec(layer, (C_CONV, SUBLANES, W_C)),
            vec(W_C),
            vec(W_C),
            vec(W_C),
            _whole_spec((D_MIX, D_MODEL)),
            vec(D_MODEL),
        ] + c_in,
        out_specs=[pl.BlockSpec((None, tt, D_MODEL), lambda b, t: (b, t, 0))] + c_out,
        scratch_shapes=[
            pltpu.VMEM((LX_HALO + tt, W_A), F32),
            pltpu.VMEM((len(_residues(LX_HALO, LRU_CONV)), LX_HALO + tt, W_A), F32),
            pltpu.VMEM((GLU_HALO + tt, W_C), F32),
            pltpu.VMEM((len(_residues(GLU_HALO, C_CONV)), GLU_HALO + tt, W_C), F32),
            pltpu.VMEM((4, BLK + tt, LANES), BF16),
            pltpu.VMEM((4, BLK + tt, LANES), BF16),
            pltpu.VMEM((1, W_A), F32),
            pltpu.VMEM((W_A // LANES, tt, LANES), F32),
            pltpu.VMEM((W_A // LANES, tt, LANES), F32),
            pltpu.VMEM((tt, W_B), F32),
            pltpu.VMEM((tt, D_MIX), BF16),
        ],
        compiler_params=pltpu.CompilerParams(
            dimension_semantics=("arbitrary", "arbitrary"), vmem_limit_bytes=VMEM_LIMIT_BYTES),
        name="mixer",
    )(sinks, x, w_in, lcw, lcb, wg, bg, lam, cw, cb, lng, lnb, w_out, post_g, *c_args)
    return out, cast


def _block_diag(w):
    eye = jnp.eye(A_BLOCKS, dtype=w.dtype)
    return jnp.einsum("lhij,hg->lhigj", w, eye).reshape(w.shape[0], W_A, W_A)


def _pick_tile(n, target):
    t = min(n, target)
    while n % t:
        t //= 2
    return t


def kernel(x, ffn1_pre_g, ffn1_w_gu, ffn1_w_down, ffn1_post_g, mix_pre_g, w_in, lru_conv_w, lru_conv_b, lru_w_a, lru_b_a, lru_w_x, lru_b_x, lru_lambda, attn_sinks, conv_w, conv_b, conv_ln_g, conv_ln_b, group_g, w_out, mix_post_g, ffn2_pre_g, ffn2_w_gu, ffn2_w_down, ffn2_post_g):
    bsz, s, d = x.shape
    depth = w_in.shape[0]
    tm = _pick_tile(bsz * s, 1024)
    tt = _pick_tile(s, 1024)
    rows = lambda v: v.reshape(depth, 1, -1)
    col = lambda g: g.reshape(depth, -1, 1)
    taps = lambda w: jnp.broadcast_to(w[:, :, None, :], w.shape[:2] + (SUBLANES,) + w.shape[2:])
    w_gates = jnp.concatenate([_block_diag(lru_w_a), _block_diag(lru_w_x)], axis=2).astype(BF16)
    b_gates = rows(jnp.concatenate([lru_b_a, lru_b_x], axis=1))
    mix_small = (taps(lru_conv_w), rows(lru_conv_b), w_gates, b_gates, rows(lru_lambda), taps(conv_w),
                 rows(conv_b), rows(conv_ln_g), rows(conv_ln_b))
    mixer_jobs = [_CastJob(w_in, col(mix_pre_g)), _CastJob(w_out, col(group_g))]
    ffn1_jobs = [_CastJob(ffn1_w_gu, col(ffn1_pre_g)), _CastJob(ffn1_w_down, None)]
    ffn2_jobs = [_CastJob(ffn2_w_gu, col(ffn2_pre_g)), _CastJob(ffn2_w_down, None)]
    w_gu = (ffn1_pre_g[0][:, None] * ffn1_w_gu[0]).astype(BF16)
    w_down = ffn1_w_down[0].astype(BF16)
    x2d = x.reshape(bsz * s, d)
    for l in range(depth):
        x2d, (w_in_l, w_out_l, w_gu, w_down) = _ffn_call(x2d, l, w_gu, w_down, rows(ffn1_post_g), tm, l,
                                                         mixer_jobs + ffn2_jobs)
        x3d, _ = _mixer_call(x2d.reshape(bsz, s, d), l, attn_sinks, w_in_l, *mix_small, w_out_l,
                             rows(mix_post_g), tt, [])
        nxt = ffn1_jobs if l + 1 < depth else []
        x2d, cast = _ffn_call(x3d.reshape(bsz * s, d), l, w_gu, w_down, rows(ffn2_post_g), tm, l + 1, nxt)
        if nxt:
            w_gu, w_down = cast
    return x2d.reshape(bsz, s, d)
```

```python
import functools
import math
from typing import NamedTuple, Optional

import jax
import jax.numpy as jnp
from jax import lax
from jax.experimental import pallas as pl
from jax.experimental.pallas import tpu as pltpu

D_MODEL = 1024
D_FF = 2816
HEAD_DIM = 64
W_A = 256
A_BLOCKS = 4
A_BLOCK_W = W_A // A_BLOCKS
LRU_C = 8.0
LRU_CONV = 4
N_Q_HEADS = 8
N_KV_HEADS = 2
W_B = N_Q_HEADS * HEAD_DIM
WINDOW = 128
BLK = 128
W_C = 256
C_CONV = 31
D_MIX = W_A + W_B + W_C
OFF_Q = 2 * W_A
OFF_K = OFF_Q + W_B
OFF_V = OFF_K + N_KV_HEADS * HEAD_DIM
OFF_GLU = OFF_V + N_KV_HEADS * HEAD_DIM
D_IN_PROJ = OFF_GLU + 2 * W_C
NORM_EPS = 1e-6
LN_EPS = 1e-5
NEG_BIG = -1e30
LOG2_E = math.log2(math.e)

SUBLANES = 8
BF16_SUBLANES = 16
LANES = 128
MXU_DIM = 256

FF_CHUNK = MXU_DIM
N_FF_CHUNKS = D_FF // FF_CHUNK
DOWN_PIECES = 4
PROJ_PIECES = 4
LX_HALO = SUBLANES
GLU_HALO = 4 * SUBLANES
CONV_ROWS = 64
VMEM_LIMIT_BYTES = 52 * 1024 * 1024

F32 = jnp.float32
BF16 = jnp.bfloat16


def _unit_rms(x):
    return x * lax.rsqrt(jnp.mean(x * x, axis=-1, keepdims=True) + NORM_EPS)


def _dot(a, b):
    return jnp.dot(a, b, preferred_element_type=F32)


def _layer_spec(layer, shape):
    idx = (layer,) + (0,) * len(shape)
    return pl.BlockSpec((None,) + tuple(shape), lambda *_: idx, pipeline_mode=pl.Buffered(1))


def _whole_spec(shape):
    zeros = (0,) * len(shape)
    return pl.BlockSpec(tuple(shape), lambda *_: zeros, pipeline_mode=pl.Buffered(1))


class _CastJob(NamedTuple):
    w: jax.Array
    gain: Optional[jax.Array]


def _cast_plumbing(jobs, layer, n_steps, step_of):
    in_specs, operands, out_specs, out_shapes = [], [], [], []
    for job in jobs:
        _, rows, cols = job.w.shape
        slab = min(r for r in range(BF16_SUBLANES, rows + 1, BF16_SUBLANES)
                   if rows % r == 0 and rows // r <= n_steps)
        block = functools.partial(lambda last, *i: jnp.minimum(step_of(*i), last), rows // slab - 1)
        in_specs.append(pl.BlockSpec((None, slab, cols), lambda *i, block=block: (layer, block(*i), 0)))
        operands.append(job.w)
        if job.gain is not None:
            in_specs.append(pl.BlockSpec((None, slab, 1), lambda *i, block=block: (layer, block(*i), 0)))
            operands.append(job.gain)
        out_specs.append(pl.BlockSpec((slab, cols), lambda *i, block=block: (block(*i), 0)))
        out_shapes.append(jax.ShapeDtypeStruct((rows, cols), BF16))
    return in_specs, operands, out_specs, out_shapes, tuple(job.gain is not None for job in jobs)


def _run_casts(has_gain, in_refs, out_refs):
    in_refs = list(in_refs)
    for gained, out_ref in zip(has_gain, out_refs):
        w = in_refs.pop(0)[...]
        if gained:
            w = w * in_refs.pop(0)[...]
        out_ref[...] = w.astype(BF16)


def _ffn_kernel(has_gain, x_ref, wgu_ref, wd_ref, post_ref, *refs):
    n_in = len(has_gain) + sum(has_gain)
    cast_in, o_ref, cast_out, act_ref = refs[:n_in], refs[n_in], refs[n_in + 1:-1], refs[-1]
    _run_casts(has_gain, cast_in, cast_out)
    x = x_ref[...]
    xb = x.astype(BF16)
    r = lax.rsqrt(jnp.mean(x * x, axis=-1, keepdims=True) + NORM_EPS)
    for j in range(N_FF_CHUNKS):
        g = _dot(xb, wgu_ref[:, j * FF_CHUNK:(j + 1) * FF_CHUNK]) * r
        u = _dot(xb, wgu_ref[:, D_FF + j * FF_CHUNK:D_FF + (j + 1) * FF_CHUNK]) * r
        act_ref[:, j * FF_CHUNK:(j + 1) * FF_CHUNK] = (g * jax.nn.sigmoid(g) * u).astype(BF16)
    rows = x.shape[0] // DOWN_PIECES
    half_gain = 0.5 * post_ref[...]
    for i in range(DOWN_PIECES):
        out = _dot(act_ref[i * rows:(i + 1) * rows, :], wd_ref[...])
        o_ref[i * rows:(i + 1) * rows, :] = x_ref[i * rows:(i + 1) * rows, :] + _unit_rms(out) * half_gain


def _ffn_call(x2d, layer, w_gu, w_down, post_g, tm, cast_layer, cast_jobs):
    n_tiles = x2d.shape[0] // tm
    c_in, c_args, c_out, c_shapes, has_gain = _cast_plumbing(cast_jobs, cast_layer, n_tiles, lambda i: i)
    out, *cast = pl.pallas_call(
        functools.partial(_ffn_kernel, has_gain),
        out_shape=[jax.ShapeDtypeStruct(x2d.shape, F32)] + c_shapes,
        grid=(n_tiles,),
        in_specs=[
            pl.BlockSpec((tm, D_MODEL), lambda i: (i, 0)),
            _whole_spec((D_MODEL, 2 * D_FF)),
            _whole_spec((D_FF, D_MODEL)),
            _layer_spec(layer, (1, D_MODEL)),
        ] + c_in,
        out_specs=[pl.BlockSpec((tm, D_MODEL), lambda i: (i, 0))] + c_out,
        scratch_shapes=[pltpu.VMEM((tm, D_FF), BF16)],
        compiler_params=pltpu.CompilerParams(
            dimension_semantics=("arbitrary",), vmem_limit_bytes=VMEM_LIMIT_BYTES),
        name="ffn",
    )(x2d, w_gu, w_down, post_g, *c_args)
    return out, cast


def _lru_scan(a, u, h0, a_s, b_s):
    t = a.shape[0]
    n_groups = t // SUBLANES
    halves = []
    lasts = []
    for c in range(W_A // LANES):
        a_s[c] = a[:, c * LANES:(c + 1) * LANES]
        b_s[c] = u[:, c * LANES:(c + 1) * LANES]
        pa = pu = None
        for r in range(SUBLANES):
            rows_r = pl.ds(r, n_groups, stride=SUBLANES)
            ar = a_s[c, rows_r, :]
            ur = b_s[c, rows_r, :]
            if r == 0:
                pa, pu = ar, ur
            else:
                pu = ar * pu + ur
                pa = ar * pa
                a_s[c, rows_r, :] = pa
                b_s[c, rows_r, :] = pu
        h = h0[:, c * LANES:(c + 1) * LANES]
        for g in range(n_groups):
            i = g * SUBLANES
            hg = a_s[c, i:i + SUBLANES, :] * h + b_s[c, i:i + SUBLANES, :]
            b_s[c, i:i + SUBLANES, :] = hg
            h = hg[SUBLANES - 1:SUBLANES, :]
        halves.append(b_s[c])
        lasts.append(h)
    return jnp.concatenate(halves, axis=-1), jnp.concatenate(lasts, axis=-1)


def _residues(halo, n_taps):
    return tuple(sorted({(halo - (n_taps - 1) + k) % SUBLANES for k in range(n_taps)} - {0}))


def _fill_shifted(ext_ref, sh_ref, residues):
    ext = ext_ref[...]
    n_rows = ext.shape[0]
    for n, s in enumerate(residues):
        sh_ref[n] = pltpu.roll(ext, n_rows - s, axis=0)


def _causal_dwconv(ext_ref, sh_ref, residues, w_ref, bias, halo, n_taps, r0, rows):
    acc = bias
    for k in range(n_taps):
        off = halo - (n_taps - 1) + k
        s = off % SUBLANES
        start = r0 + off - s
        if s == 0:
            tap = ext_ref[start:start + rows, :]
        else:
            tap = sh_ref[residues.index(s), start:start + rows, :]
        acc = acc + jnp.concatenate([w_ref[k]] * (rows // SUBLANES), axis=0) * tap
    return acc


def _gelu_tanh(x):
    return 0.5 * x * (1.0 + jnp.tanh(math.sqrt(2.0 / math.pi) * (x + 0.044715 * (x * x * x))))


def _mixer_kernel(layer, has_gain, sinks_ref, x_ref, w_in_ref, lcw_ref, lcb_ref, wg_ref, bg_ref, lam_ref,
                  cw_ref, cb_ref, lng_ref, lnb_ref, w_out_ref, post_ref, *refs):
    n_in = len(has_gain) + sum(has_gain)
    n_out = len(has_gain)
    cast_in, o_ref, cast_out = refs[:n_in], refs[n_in], refs[n_in + 1:n_in + 1 + n_out]
    lx_ext, lx_sh, glu_ext, glu_sh, kx, vx, h_ref, a_s, b_s, yb_ref, ycat_ref = refs[n_in + 1 + n_out:]
    _run_casts(has_gain, cast_in, cast_out)
    t_idx = pl.program_id(1)
    tt = x_ref.shape[0]

    @pl.when(t_idx == 0)
    def _():
        lx_ext[0:LX_HALO, :] = jnp.zeros((LX_HALO, W_A), F32)
        glu_ext[0:GLU_HALO, :] = jnp.zeros((GLU_HALO, W_C), F32)
        kx[:, 0:BLK, :] = jnp.zeros((4, BLK, LANES), BF16)
        vx[:, 0:BLK, :] = jnp.zeros((4, BLK, LANES), BF16)
        h_ref[...] = jnp.zeros((1, W_A), F32)

    rows = tt // PROJ_PIECES
    proj = jnp.concatenate([_dot(_unit_rms(x_ref[i * rows:(i + 1) * rows, :]).astype(BF16), w_in_ref[...])
                            for i in range(PROJ_PIECES)], axis=0)

    gate = proj[:, W_A:OFF_Q]
    lx_ext[LX_HALO:LX_HALO + tt, :] = proj[:, :W_A]
    lx_res = _residues(LX_HALO, LRU_CONV)
    _fill_shifted(lx_ext, lx_sh, lx_res)
    xc = _causal_dwconv(lx_ext, lx_sh, lx_res, lcw_ref, lcb_ref[...], LX_HALO, LRU_CONV, 0, tt)
    lx_ext[0:LX_HALO, :] = lx_ext[tt:tt + LX_HALO, :]
    gr = _dot(xc.astype(BF16), wg_ref[...]) + bg_ref[...]
    r = jax.nn.sigmoid(gr[:, :W_A])
    i_gate = jax.nn.sigmoid(gr[:, W_A:])
    z = -lam_ref[...]
    softplus = jnp.maximum(z, 0.0) + jnp.log1p(jnp.exp(-jnp.abs(z)))
    log_a = (-LRU_C) * r * softplus
    a = jnp.exp(log_a)
    th = jnp.tanh(log_a)
    u = jnp.sqrt(-2.0 * th / (1.0 - th)) * (i_gate * xc)
    h, h_last = _lru_scan(a, u, h_ref[...], a_s, b_s)
    h_ref[...] = h_last
    ycat_ref[:, 0:W_A] = _unit_rms(_gelu_tanh(gate) * h).astype(BF16)

    glu_ext[GLU_HALO:GLU_HALO + tt, :] = (proj[:, OFF_GLU:OFF_GLU + W_C]
                                          * jax.nn.sigmoid(proj[:, OFF_GLU + W_C:]))
    glu_res = _residues(GLU_HALO, C_CONV)
    _fill_shifted(glu_ext, glu_sh, glu_res)

    def conformer_rows(r0):
        yc = _causal_dwconv(glu_ext, glu_sh, glu_res, cw_ref, cb_ref[...], GLU_HALO, C_CONV, r0, CONV_ROWS)
        mu = jnp.mean(yc, axis=-1, keepdims=True)
        ycc = yc - mu
        var = jnp.mean(ycc * ycc, axis=-1, keepdims=True)
        yc = ycc * lax.rsqrt(var + LN_EPS) * lng_ref[...] + lnb_ref[...]
        y_c = yc * jax.nn.sigmoid(yc)
        ycat_ref[r0:r0 + CONV_ROWS, W_A + W_B:] = _unit_rms(y_c).astype(BF16)

    q = (proj[:, OFF_Q:OFF_K] * (LOG2_E / math.sqrt(HEAD_DIM))).astype(BF16)
    lane = lax.broadcasted_iota(jnp.int32, (tt, LANES), 1)
    lo = lane < HEAD_DIM
    for src, dst in ((proj[:, OFF_K:OFF_V], kx), (proj[:, OFF_V:OFF_GLU], vx)):
        rot = pltpu.roll(src, HEAD_DIM, axis=1)
        dst[0, BLK:BLK + tt, :] = jnp.where(lo, src, 0.0).astype(BF16)
        dst[1, BLK:BLK + tt, :] = jnp.where(lo, 0.0, rot).astype(BF16)
        dst[2, BLK:BLK + tt, :] = jnp.where(lo, rot, 0.0).astype(BF16)
        dst[3, BLK:BLK + tt, :] = jnp.where(lo, 0.0, src).astype(BF16)

    qi = lax.broadcasted_iota(jnp.int32, (2 * BLK, BLK), 0) & (BLK - 1)
    kc = lax.broadcasted_iota(jnp.int32, (2 * BLK, BLK), 1)
    from_prev = kc > qi
    top = lax.broadcasted_iota(jnp.int32, (2 * BLK, 1), 0) < BLK
    for j in range(tt // BLK):
        no_prev = from_prev & (t_idx == 0) if j == 0 else None
        for g in range(N_KV_HEADS):
            p0 = 2 * g * LANES
            lhs = jnp.concatenate([q[j * BLK:(j + 1) * BLK, p0:p0 + LANES],
                                   q[j * BLK:(j + 1) * BLK, p0 + LANES:p0 + 2 * LANES]], axis=0)
            o = jnp.zeros((2 * BLK, LANES), F32)
            for e in range(2):
                kk = kx[2 * g + e, j * BLK:(j + 2) * BLK, :]
                vv = vx[2 * g + e, j * BLK:(j + 2) * BLK, :]
                s = lax.dot_general(lhs, kk, (((1,), (1,)), ((), ())), preferred_element_type=F32)
                s = jnp.where(from_prev, s[:, :BLK], s[:, BLK:])
                if no_prev is not None:
                    s = jnp.where(no_prev, NEG_BIG, s)
                sink = jnp.where(top, sinks_ref[layer, 4 * g + e], sinks_ref[layer, 4 * g + 2 + e]) * LOG2_E
                m = jnp.max(s, axis=-1, keepdims=True)
                p = jnp.exp2(s - m)
                denom = jnp.sum(p, axis=-1, keepdims=True) + jnp.exp2(sink - m)
                p = jnp.concatenate([jnp.where(from_prev, p, 0.0), jnp.where(from_prev, 0.0, p)], axis=-1)
                o = o + _dot(p.astype(BF16), vv) * (1.0 / denom)
            yb_ref[j * BLK:(j + 1) * BLK, p0:p0 + LANES] = o[:BLK]
            yb_ref[j * BLK:(j + 1) * BLK, p0 + LANES:p0 + 2 * LANES] = o[BLK:]
        for r0 in range(j * BLK, (j + 1) * BLK, CONV_ROWS):
            conformer_rows(r0)
    kx[:, 0:BLK, :] = kx[:, tt:tt + BLK, :]
    vx[:, 0:BLK, :] = vx[:, tt:tt + BLK, :]
    glu_ext[0:GLU_HALO, :] = glu_ext[tt:tt + GLU_HALO, :]
    ycat_ref[:, W_A:W_A + W_B] = _unit_rms(yb_ref[...]).astype(BF16)

    for i in range(PROJ_PIECES):
        piece = slice(i * rows, (i + 1) * rows)
        o_ref[piece, :] = x_ref[piece, :] + _unit_rms(_dot(ycat_ref[piece, :], w_out_ref[...])) * post_ref[...]


def _mixer_call(x, layer, sinks, w_in, lcw, lcb, wg, bg, lam, cw, cb, lng, lnb, w_out, post_g, tt, cast_jobs):
    bsz, s, _ = x.shape
    n_t = s // tt
    vec = lambda n: _layer_spec(layer, (1, n))
    c_in, c_args, c_out, c_shapes, has_gain = _cast_plumbing(cast_jobs, layer, bsz * n_t, lambda b, t: b * n_t + t)
    out, *cast = pl.pallas_call(
        functools.partial(_mixer_kernel, layer, has_gain),
        out_shape=[jax.ShapeDtypeStruct(x.shape, F32)] + c_shapes,
        grid=(bsz, n_t),
        in_specs=[
            pl.BlockSpec(memory_space=pltpu.SMEM),
            pl.BlockSpec((None, tt, D_MODEL), lambda b, t: (b, t, 0)),
            _whole_spec((D_MODEL, D_IN_PROJ)),
            _layer_spec(layer, (LRU_CONV, SUBLANES, W_A)),
            vec(W_A),
            _layer_spec(layer, (W_A, 2 * W_A)),
            vec(2 * W_A),
            vec(W_A),
            _layer_spec(layer, (C_CONV, SUBLANES, W_C)),
            vec(W_C),
            vec(W_C),
            vec(W_C),
            _whole_spec((D_MIX, D_MODEL)),
            vec(D_MODEL),
        ] + c_in,
        out_specs=[pl.BlockSpec((None, tt, D_MODEL), lambda b, t: (b, t, 0))] + c_out,
        scratch_shapes=[
            pltpu.VMEM((LX_HALO + tt, W_A), F32),
            pltpu.VMEM((len(_residues(LX_HALO, LRU_CONV)), LX_HALO + tt, W_A), F32),
            pltpu.VMEM((GLU_HALO + tt, W_C), F32),
            pltpu.VMEM((len(_residues(GLU_HALO, C_CONV)), GLU_HALO + tt, W_C), F32),
            pltpu.VMEM((4, BLK + tt, LANES), BF16),
            pltpu.VMEM((4, BLK + tt, LANES), BF16),
            pltpu.VMEM((1, W_A), F32),
            pltpu.VMEM((W_A // LANES, tt, LANES), F32),
            pltpu.VMEM((W_A // LANES, tt, LANES), F32),
            pltpu.VMEM((tt, W_B), F32),
            pltpu.VMEM((tt, D_MIX), BF16),
        ],
        compiler_params=pltpu.CompilerParams(
            dimension_semantics=("arbitrary", "arbitrary"), vmem_limit_bytes=VMEM_LIMIT_BYTES),
        name="mixer",
    )(sinks, x, w_in, lcw, lcb, wg, bg, lam, cw, cb, lng, lnb, w_out, post_g, *c_args)
    return out, cast


def _block_diag(w):
    eye = jnp.eye(A_BLOCKS, dtype=w.dtype)
    return jnp.einsum("lhij,hg->lhigj", w, eye).reshape(w.shape[0], W_A, W_A)


def _pick_tile(n, target):
    t = min(n, target)
    while n % t:
        t //= 2
    return t


def kernel(x, ffn1_pre_g, ffn1_w_gu, ffn1_w_down, ffn1_post_g, mix_pre_g, w_in, lru_conv_w, lru_conv_b, lru_w_a, lru_b_a, lru_w_x, lru_b_x, lru_lambda, attn_sinks, conv_w, conv_b, conv_ln_g, conv_ln_b, group_g, w_out, mix_post_g, ffn2_pre_g, ffn2_w_gu, ffn2_w_down, ffn2_post_g):
    bsz, s, d = x.shape
    depth = w_in.shape[0]
    tm = _pick_tile(bsz * s, 1024)
    tt = _pick_tile(s, 1024)
    rows = lambda v: v.reshape(depth, 1, -1)
    col = lambda g: g.reshape(depth, -1, 1)
    taps = lambda w: jnp.broadcast_to(w[:, :, None, :], w.shape[:2] + (SUBLANES,) + w.shape[2:])
    w_gates = jnp.concatenate([_block_diag(lru_w_a), _block_diag(lru_w_x)], axis=2).astype(BF16)
    b_gates = rows(jnp.concatenate([lru_b_a, lru_b_x], axis=1))
    mix_small = (taps(lru_conv_w), rows(lru_conv_b), w_gates, b_gates, rows(lru_lambda), taps(conv_w),
                 rows(conv_b), rows(conv_ln_g), rows(conv_ln_b))
    mixer_jobs = [_CastJob(w_in, col(mix_pre_g)), _CastJob(w_out, col(group_g))]
    ffn1_jobs = [_CastJob(ffn1_w_gu, col(ffn1_pre_g)), _CastJob(ffn1_w_down, None)]
    ffn2_jobs = [_CastJob(ffn2_w_gu, col(ffn2_pre_g)), _CastJob(ffn2_w_down, None)]
    w_gu = (ffn1_pre_g[0][:, None] * ffn1_w_gu[0]).astype(BF16)
    w_down = ffn1_w_down[0].astype(BF16)
    x2d = x.reshape(bsz * s, d)
    for l in range(depth):
        x2d, (w_in_l, w_out_l, w_gu, w_down) = _ffn_call(x2d, l, w_gu, w_down, rows(ffn1_post_g), tm, l,
                                                         mixer_jobs + ffn2_jobs)
        x3d, _ = _mixer_call(x2d.reshape(bsz, s, d), l, attn_sinks, w_in_l, *mix_small, w_out_l,
                             rows(mix_post_g), tt, [])
        nxt = ffn1_jobs if l + 1 < depth else []
        x2d, cast = _ffn_call(x3d.reshape(bsz * s, d), l, w_gu, w_down, rows(ffn2_post_g), tm, l + 1, nxt)
        if nxt:
            w_gu, w_down = cast
    return x2d.reshape(bsz, s, d)
```

```python
import functools
import math
from typing import NamedTuple, Optional

import jax
import jax.numpy as jnp
from jax import lax
from jax.experimental import pallas as pl
from jax.experimental.pallas import tpu as pltpu

D_MODEL = 1024
D_FF = 2816
HEAD_DIM = 64
W_A = 256
A_BLOCKS = 4
A_BLOCK_W = W_A // A_BLOCKS
LRU_C = 8.0
LRU_CONV = 4
N_Q_HEADS = 8
N_KV_HEADS = 2
W_B = N_Q_HEADS * HEAD_DIM
WINDOW = 128
BLK = 128
W_C = 256
C_CONV = 31
D_MIX = W_A + W_B + W_C
OFF_Q = 2 * W_A
OFF_K = OFF_Q + W_B
OFF_V = OFF_K + N_KV_HEADS * HEAD_DIM
OFF_GLU = OFF_V + N_KV_HEADS * HEAD_DIM
D_IN_PROJ = OFF_GLU + 2 * W_C
NORM_EPS = 1e-6
LN_EPS = 1e-5
NEG_BIG = -1e30
LOG2_E = math.log2(math.e)

SUBLANES = 8
BF16_SUBLANES = 16
LANES = 128
MXU_DIM = 256

FF_CHUNK = MXU_DIM
N_FF_CHUNKS = D_FF // FF_CHUNK
DOWN_PIECES = 4
PROJ_PIECES = 4
LX_HALO = SUBLANES
GLU_HALO = 4 * SUBLANES
CONV_ROWS = 64
VMEM_LIMIT_BYTES = 52 * 1024 * 1024

F32 = jnp.float32
BF16 = jnp.bfloat16


def _unit_rms(x):
    return x * lax.rsqrt(jnp.mean(x * x, axis=-1, keepdims=True) + NORM_EPS)


def _dot(a, b):
    return jnp.dot(a, b, preferred_element_type=F32)


def _layer_spec(layer, shape):
    idx = (layer,) + (0,) * len(shape)
    return pl.BlockSpec((None,) + tuple(shape), lambda *_: idx, pipeline_mode=pl.Buffered(1))


def _whole_spec(shape):
    zeros = (0,) * len(shape)
    return pl.BlockSpec(tuple(shape), lambda *_: zeros, pipeline_mode=pl.Buffered(1))


class _CastJob(NamedTuple):
    w: jax.Array
    gain: Optional[jax.Array]


def _cast_plumbing(jobs, layer, n_steps, step_of):
    in_specs, operands, out_specs, out_shapes = [], [], [], []
    for job in jobs:
        _, rows, cols = job.w.shape
        slab = min(r for r in range(BF16_SUBLANES, rows + 1, BF16_SUBLANES)
                   if rows % r == 0 and rows // r <= n_steps)
        block = functools.partial(lambda last, *i: jnp.minimum(step_of(*i), last), rows // slab - 1)
        in_specs.append(pl.BlockSpec((None, slab, cols), lambda *i, block=block: (layer, block(*i), 0)))
        operands.append(job.w)
        if job.gain is not None:
            in_specs.append(pl.BlockSpec((None, slab, 1), lambda *i, block=block: (layer, block(*i), 0)))
            operands.append(job.gain)
        out_specs.append(pl.BlockSpec((slab, cols), lambda *i, block=block: (block(*i), 0)))
        out_shapes.append(jax.ShapeDtypeStruct((rows, cols), BF16))
    return in_specs, operands, out_specs, out_shapes, tuple(job.gain is not None for job in jobs)


def _run_casts(has_gain, in_refs, out_refs):
    in_refs = list(in_refs)
    for gained, out_ref in zip(has_gain, out_refs):
        w = in_refs.pop(0)[...]
        if gained:
            w = w * in_refs.pop(0)[...]
        out_ref[...] = w.astype(BF16)


def _ffn_kernel(has_gain, x_ref, wgu_ref, wd_ref, post_ref, *refs):
    n_in = len(has_gain) + sum(has_gain)
    cast_in, o_ref, cast_out, act_ref = refs[:n_in], refs[n_in], refs[n_in + 1:-1], refs[-1]
    _run_casts(has_gain, cast_in, cast_out)
    x = x_ref[...]
    xb = x.astype(BF16)
    r = lax.rsqrt(jnp.mean(x * x, axis=-1, keepdims=True) + NORM_EPS)
    for j in range(N_FF_CHUNKS):
        g = _dot(xb, wgu_ref[:, j * FF_CHUNK:(j + 1) * FF_CHUNK]) * r
        u = _dot(xb, wgu_ref[:, D_FF + j * FF_CHUNK:D_FF + (j + 1) * FF_CHUNK]) * r
        act_ref[:, j * FF_CHUNK:(j + 1) * FF_CHUNK] = (g * jax.nn.sigmoid(g) * u).astype(BF16)
    rows = x.shape[0] // DOWN_PIECES
    half_gain = 0.5 * post_ref[...]
    for i in range(DOWN_PIECES):
        out = _dot(act_ref[i * rows:(i + 1) * rows, :], wd_ref[...])
        o_ref[i * rows:(i + 1) * rows, :] = x_ref[i * rows:(i + 1) * rows, :] + _unit_rms(out) * half_gain


def _ffn_call(x2d, layer, w_gu, w_down, post_g, tm, cast_layer, cast_jobs):
    n_tiles = x2d.shape[0] // tm
    c_in, c_args, c_out, c_shapes, has_gain = _cast_plumbing(cast_jobs, cast_layer, n_tiles, lambda i: i)
    out, *cast = pl.pallas_call(
        functools.partial(_ffn_kernel, has_gain),
        out_shape=[jax.ShapeDtypeStruct(x2d.shape, F32)] + c_shapes,
        grid=(n_tiles,),
        in_specs=[
            pl.BlockSpec((tm, D_MODEL), lambda i: (i, 0)),
            _whole_spec((D_MODEL, 2 * D_FF)),
            _whole_spec((D_FF, D_MODEL)),
            _layer_spec(layer, (1, D_MODEL)),
        ] + c_in,
        out_specs=[pl.BlockSpec((tm, D_MODEL), lambda i: (i, 0))] + c_out,
        scratch_shapes=[pltpu.VMEM((tm, D_FF), BF16)],
        compiler_params=pltpu.CompilerParams(
            dimension_semantics=("arbitrary",), vmem_limit_bytes=VMEM_LIMIT_BYTES),
        name="ffn",
    )(x2d, w_gu, w_down, post_g, *c_args)
    return out, cast


def _lru_scan(a, u, h0, a_s, b_s):
    t = a.shape[0]
    n_groups = t // SUBLANES
    halves = []
    lasts = []
    for c in range(W_A // LANES):
        a_s[c] = a[:, c * LANES:(c + 1) * LANES]
        b_s[c] = u[:, c * LANES:(c + 1) * LANES]
        pa = pu = None
        for r in range(SUBLANES):
            rows_r = pl.ds(r, n_groups, stride=SUBLANES)
            ar = a_s[c, rows_r, :]
            ur = b_s[c, rows_r, :]
            if r == 0:
                pa, pu = ar, ur
            else:
                pu = ar * pu + ur
                pa = ar * pa
                a_s[c, rows_r, :] = pa
                b_s[c, rows_r, :] = pu
        h = h0[:, c * LANES:(c + 1) * LANES]
        for g in range(n_groups):
            i = g * SUBLANES
            hg = a_s[c, i:i + SUBLANES, :] * h + b_s[c, i:i + SUBLANES, :]
            b_s[c, i:i + SUBLANES, :] = hg
            h = hg[SUBLANES - 1:SUBLANES, :]
        halves.append(b_s[c])
        lasts.append(h)
    return jnp.concatenate(halves, axis=-1), jnp.concatenate(lasts, axis=-1)


def _residues(halo, n_taps):
    return tuple(sorted({(halo - (n_taps - 1) + k) % SUBLANES for k in range(n_taps)} - {0}))


def _fill_shifted(ext_ref, sh_ref, residues):
    ext = ext_ref[...]
    n_rows = ext.shape[0]
    for n, s in enumerate(residues):
        sh_ref[n] = pltpu.roll(ext, n_rows - s, axis=0)


def _causal_dwconv(ext_ref, sh_ref, residues, w_ref, bias, halo, n_taps, r0, rows):
    acc = bias
    for k in range(n_taps):
        off = halo - (n_taps - 1) + k
        s = off % SUBLANES
        start = r0 + off - s
        if s == 0:
            tap = ext_ref[start:start + rows, :]
        else:
            tap = sh_ref[residues.index(s), start:start + rows, :]
        acc = acc + jnp.concatenate([w_ref[k]] * (rows // SUBLANES), axis=0) * tap
    return acc


def _gelu_tanh(x):
    return 0.5 * x * (1.0 + jnp.tanh(math.sqrt(2.0 / math.pi) * (x + 0.044715 * (x * x * x))))


def _mixer_kernel(layer, has_gain, sinks_ref, x_ref, w_in_ref, lcw_ref, lcb_ref, wg_ref, bg_ref, lam_ref,
                  cw_ref, cb_ref, lng_ref, lnb_ref, w_out_ref, post_ref, *refs):
    n_in = len(has_gain) + sum(has_gain)
    n_out = len(has_gain)
    cast_in, o_ref, cast_out = refs[:n_in], refs[n_in], refs[n_in + 1:n_in + 1 + n_out]
    lx_ext, lx_sh, glu_ext, glu_sh, kx, vx, h_ref, a_s, b_s, yb_ref, ycat_ref = refs[n_in + 1 + n_out:]
    _run_casts(has_gain, cast_in, cast_out)
    t_idx = pl.program_id(1)
    tt = x_ref.shape[0]

    @pl.when(t_idx == 0)
    def _():
        lx_ext[0:LX_HALO, :] = jnp.zeros((LX_HALO, W_A), F32)
        glu_ext[0:GLU_HALO, :] = jnp.zeros((GLU_HALO, W_C), F32)
        kx[:, 0:BLK, :] = jnp.zeros((4, BLK, LANES), BF16)
        vx[:, 0:BLK, :] = jnp.zeros((4, BLK, LANES), BF16)
        h_ref[...] = jnp.zeros((1, W_A), F32)

    rows = tt // PROJ_PIECES
    proj = jnp.concatenate([_dot(_unit_rms(x_ref[i * rows:(i + 1) * rows, :]).astype(BF16), w_in_ref[...])
                            for i in range(PROJ_PIECES)], axis=0)

    gate = proj[:, W_A:OFF_Q]
    lx_ext[LX_HALO:LX_HALO + tt, :] = proj[:, :W_A]
    lx_res = _residues(LX_HALO, LRU_CONV)
    _fill_shifted(lx_ext, lx_sh, lx_res)
    xc = _causal_dwconv(lx_ext, lx_sh, lx_res, lcw_ref, lcb_ref[...], LX_HALO, LRU_CONV, 0, tt)
    lx_ext[0:LX_HALO, :] = lx_ext[tt:tt + LX_HALO, :]
    gr = _dot(xc.astype(BF16), wg_ref[...]) + bg_ref[...]
    r = jax.nn.sigmoid(gr[:, :W_A])
    i_gate = jax.nn.sigmoid(gr[:, W_A:])
    z = -lam_ref[...]
    softplus = jnp.maximum(z, 0.0) + jnp.log1p(jnp.exp(-jnp.abs(z)))
    log_a = (-LRU_C) * r * softplus
    a = jnp.exp(log_a)
    th = jnp.tanh(log_a)
    u = jnp.sqrt(-2.0 * th / (1.0 - th)) * (i_gate * xc)
    h, h_last = _lru_scan(a, u, h_ref[...], a_s, b_s)
    h_ref[...] = h_last
    ycat_ref[:, 0:W_A] = _unit_rms(_gelu_tanh(gate) * h).astype(BF16)

    glu_ext[GLU_HALO:GLU_HALO + tt, :] = (proj[:, OFF_GLU:OFF_GLU + W_C]
                                          * jax.nn.sigmoid(proj[:, OFF_GLU + W_C:]))
    glu_res = _residues(GLU_HALO, C_CONV)
    _fill_shifted(glu_ext, glu_sh, glu_res)

    def conformer_rows(r0):
        yc = _causal_dwconv(glu_ext, glu_sh, glu_res, cw_ref, cb_ref[...], GLU_HALO, C_CONV, r0, CONV_ROWS)
        mu = jnp.mean(yc, axis=-1, keepdims=True)
        ycc = yc - mu
        var = jnp.mean(ycc * ycc, axis=-1, keepdims=True)
        yc = ycc * lax.rsqrt(var + LN_EPS) * lng_ref[...] + lnb_ref[...]
        y_c = yc * jax.nn.sigmoid(yc)
        ycat_ref[r0:r0 + CONV_ROWS, W_A + W_B:] = _unit_rms(y_c).astype(BF16)

    q = (proj[:, OFF_Q:OFF_K] * (LOG2_E / math.sqrt(HEAD_DIM))).astype(BF16)
    lane = lax.broadcasted_iota(jnp.int32, (tt, LANES), 1)
    lo = lane < HEAD_DIM
    for src, dst in ((proj[:, OFF_K:OFF_V], kx), (proj[:, OFF_V:OFF_GLU], vx)):
        rot = pltpu.roll(src, HEAD_DIM, axis=1)
        dst[0, BLK:BLK + tt, :] = jnp.where(lo, src, 0.0).astype(BF16)
        dst[1, BLK:BLK + tt, :] = jnp.where(lo, 0.0, rot).astype(BF16)
        dst[2, BLK:BLK + tt, :] = jnp.where(lo, rot, 0.0).astype(BF16)
        dst[3, BLK:BLK + tt, :] = jnp.where(lo, 0.0, src).astype(BF16)

    qi = lax.broadcasted_iota(jnp.int32, (BLK, BLK), 0)
    kc = lax.broadcasted_iota(jnp.int32, (BLK, BLK), 1)
    from_prev = kc > qi
    for j in range(tt // BLK):
        no_prev = from_prev & (t_idx == 0) if j == 0 else None
        for pair in range(N_Q_HEADS // 2):
            g = pair // 2
            p0 = pair * LANES
            lhs = q[j * BLK:(j + 1) * BLK, p0:p0 + LANES]
            o = jnp.zeros((BLK, LANES), F32)
            for e in range(2):
                kk = kx[2 * g + e, j * BLK:(j + 2) * BLK, :]
                vv = vx[2 * g + e, j * BLK:(j + 2) * BLK, :]
                s = lax.dot_general(lhs, kk, (((1,), (1,)), ((), ())), preferred_element_type=F32)
                s = jnp.where(from_prev, s[:, :BLK], s[:, BLK:])
                if no_prev is not None:
                    s = jnp.where(no_prev, NEG_BIG, s)
                sink = sinks_ref[layer, 2 * pair + e] * LOG2_E
                m = jnp.max(s, axis=-1, keepdims=True)
                p = jnp.exp2(s - m)
                denom = jnp.sum(p, axis=-1, keepdims=True) + jnp.exp2(sink - m)
                p = jnp.concatenate([jnp.where(from_prev, p, 0.0), jnp.where(from_prev, 0.0, p)], axis=-1)
                o = o + _dot(p.astype(BF16), vv) * (1.0 / denom)
            yb_ref[j * BLK:(j + 1) * BLK, p0:p0 + LANES] = o
        for r0 in range(j * BLK, (j + 1) * BLK, CONV_ROWS):
            conformer_rows(r0)
    kx[:, 0:BLK, :] = kx[:, tt:tt + BLK, :]
    vx[:, 0:BLK, :] = vx[:, tt:tt + BLK, :]
    glu_ext[0:GLU_HALO, :] = glu_ext[tt:tt + GLU_HALO, :]
    ycat_ref[:, W_A:W_A + W_B] = _unit_rms(yb_ref[...]).astype(BF16)

    for i in range(PROJ_PIECES):
        piece = slice(i * rows, (i + 1) * rows)
        o_ref[piece, :] = x_ref[piece, :] + _unit_rms(_dot(ycat_ref[piece, :], w_out_ref[...])) * post_ref[...]


def _mixer_call(x, layer, sinks, w_in, lcw, lcb, wg, bg, lam, cw, cb, lng, lnb, w_out, post_g, tt, cast_jobs):
    bsz, s, _ = x.shape
    n_t = s // tt
    vec = lambda n: _layer_spec(layer, (1, n))
    c_in, c_args, c_out, c_shapes, has_gain = _cast_plumbing(cast_jobs, layer, bsz * n_t, lambda b, t: b * n_t + t)
    out, *cast = pl.pallas_call(
        functools.partial(_mixer_kernel, layer, has_gain),
        out_shape=[jax.ShapeDtypeStruct(x.shape, F32)] + c_shapes,
        grid=(bsz, n_t),
        in_specs=[
            pl.BlockSpec(memory_space=pltpu.SMEM),
            pl.BlockSpec((None, tt, D_MODEL), lambda b, t: (b, t, 0)),
            _whole_spec((D_MODEL, D_IN_PROJ)),
            _layer_spec(layer, (LRU_CONV, SUBLANES, W_A)),
            vec(W_A),
            _layer_spec(layer, (W_A, 2 * W_A)),
            vec(2 * W_A),
            vec(W_A),
            _layer_spec(layer, (C_CONV, SUBLANES, W_C)),
            vec(W_C),
            vec(W_C),
            vec(W_C),
            _whole_spec((D_MIX, D_MODEL)),
            vec(D_MODEL),
        ] + c_in,
        out_specs=[pl.BlockSpec((None, tt, D_MODEL), lambda b, t: (b, t, 0))] + c_out,
        scratch_shapes=[
            pltpu.VMEM((LX_HALO + tt, W_A), F32),
            pltpu.VMEM((len(_residues(LX_HALO, LRU_CONV)), LX_HALO + tt, W_A), F32),
            pltpu.VMEM((GLU_HALO + tt, W_C), F32),
            pltpu.VMEM((len(_residues(GLU_HALO, C_CONV)), GLU_HALO + tt, W_C), F32),
            pltpu.VMEM((4, BLK + tt, LANES), BF16),
            pltpu.VMEM((4, BLK + tt, LANES), BF16),
            pltpu.VMEM((1, W_A), F32),
            pltpu.VMEM((W_A // LANES, tt, LANES), F32),
            pltpu.VMEM((W_A // LANES, tt, LANES), F32),
            pltpu.VMEM((tt, W_B), F32),
            pltpu.VMEM((tt, D_MIX), BF16),
        ],
        compiler_params=pltpu.CompilerParams(
            dimension_semantics=("arbitrary", "arbitrary"), vmem_limit_bytes=VMEM_LIMIT_BYTES),
        name="mixer",
    )(sinks, x, w_in, lcw, lcb, wg, bg, lam, cw, cb, lng, lnb, w_out, post_g, *c_args)
    return out, cast


def _block_diag(w):
    eye = jnp.eye(A_BLOCKS, dtype=w.dtype)
    return jnp.einsum("lhij,hg->lhigj", w, eye).reshape(w.shape[0], W_A, W_A)


def _pick_tile(n, target):
    t = min(n, target)
    while n % t:
        t //= 2
    return t


def kernel(x, ffn1_pre_g, ffn1_w_gu, ffn1_w_down, ffn1_post_g, mix_pre_g, w_in, lru_conv_w, lru_conv_b, lru_w_a, lru_b_a, lru_w_x, lru_b_x, lru_lambda, attn_sinks, conv_w, conv_b, conv_ln_g, conv_ln_b, group_g, w_out, mix_post_g, ffn2_pre_g, ffn2_w_gu, ffn2_w_down, ffn2_post_g):
    bsz, s, d = x.shape
    depth = w_in.shape[0]
    tm = _pick_tile(bsz * s, 1024)
    tt = _pick_tile(s, 1024)
    rows = lambda v: v.reshape(depth, 1, -1)
    col = lambda g: g.reshape(depth, -1, 1)
    taps = lambda w: jnp.broadcast_to(w[:, :, None, :], w.shape[:2] + (SUBLANES,) + w.shape[2:])
    w_gates = jnp.concatenate([_block_diag(lru_w_a), _block_diag(lru_w_x)], axis=2).astype(BF16)
    b_gates = rows(jnp.concatenate([lru_b_a, lru_b_x], axis=1))
    mix_small = (taps(lru_conv_w), rows(lru_conv_b), w_gates, b_gates, rows(lru_lambda), taps(conv_w),
                 rows(conv_b), rows(conv_ln_g), rows(conv_ln_b))
    mixer_jobs = [_CastJob(w_in, col(mix_pre_g)), _CastJob(w_out, col(group_g))]
    ffn1_jobs = [_CastJob(ffn1_w_gu, col(ffn1_pre_g)), _CastJob(ffn1_w_down, None)]
    ffn2_jobs = [_CastJob(ffn2_w_gu, col(ffn2_pre_g)), _CastJob(ffn2_w_down, None)]
    w_gu = (ffn1_pre_g[0][:, None] * ffn1_w_gu[0]).astype(BF16)
    w_down = ffn1_w_down[0].astype(BF16)
    x2d = x.reshape(bsz * s, d)
    for l in range(depth):
        x2d, (w_in_l, w_out_l, w_gu, w_down) = _ffn_call(x2d, l, w_gu, w_down, rows(ffn1_post_g), tm, l,
                                                         mixer_jobs + ffn2_jobs)
        x3d, _ = _mixer_call(x2d.reshape(bsz, s, d), l, attn_sinks, w_in_l, *mix_small, w_out_l,
                             rows(mix_post_g), tt, [])
        nxt = ffn1_jobs if l + 1 < depth else []
        x2d, cast = _ffn_call(x3d.reshape(bsz * s, d), l, w_gu, w_down, rows(ffn2_post_g), tm, l + 1, nxt)
        if nxt:
            w_gu, w_down = cast
    return x2d.reshape(bsz, s, d)
```
